```python
import math
import jax
import jax.numpy as jnp
from jax import lax
import numpy as np

D_MODEL = 1024
BATCH = 2
SEQ = 8192
DEPTH = 2
DEC_BATCH = 128
DEC_SEQ = 1
PAST_LEN = 2048
PAGE_SIZE = 128

HEAD_DIM = 64
HEADS_PER_GROUP = 4
ATTN_CONFIGS = ((128, 1), (512, 4), (2048, 16))
N_ATTN_HEADS = HEADS_PER_GROUP * len(ATTN_CONFIGS)
ATTN_WIDTH = N_ATTN_HEADS * HEAD_DIM
ATTN_OUT_WIDTH = HEADS_PER_GROUP * HEAD_DIM
ROPE_DIM = HEAD_DIM // 4
ROPE_THETA = 500000.0
BLK = 128
SSM_CH = 16
SSM_GROUPS = 32
SSM_WIDTH = SSM_CH * SSM_GROUPS
SSM_STATE = 64
PEER_HEADS = 8
PEER_NKEYS = 128
PEER_EXPERTS = PEER_NKEYS * PEER_NKEYS
PEER_QDIM = 256
PEER_HALF = PEER_QDIM // 2
PEER_TOPK = 16
PEER_BLOCK = 128
IN_WIDTH = 3 * ATTN_WIDTH + SSM_WIDTH + 2 * D_MODEL
EPS = 1e-6
NEG = -1e30

kernel_name = 'dilated_s5_peer_hybrid_step'


def rmsnorm(x, w):
    xf = x.astype(jnp.float32)
    y = xf * lax.rsqrt(jnp.mean(xf * xf, axis=-1, keepdims=True) + EPS)
    return (y * w.astype(jnp.float32)).astype(x.dtype)


def unit_rms(x):
    xf = x.astype(jnp.float32)
    return xf * lax.rsqrt(jnp.mean(xf * xf, axis=-1, keepdims=True) + EPS)


def partial_rope(x, pos):
    half = ROPE_DIM // 2
    inv = ROPE_THETA ** (-jnp.arange(half, dtype=jnp.float32) / half)
    ang = pos.astype(jnp.float32)[:, None] * inv[None, :]
    cos = jnp.cos(ang)[:, None, :]
    sin = jnp.sin(ang)[:, None, :]
    xr = x[..., :ROPE_DIM].astype(jnp.float32)
    x1, x2 = xr[..., :half], xr[..., half:]
    rot = jnp.concatenate([x1 * cos - x2 * sin, x2 * cos + x1 * sin], axis=-1)
    return jnp.concatenate([rot.astype(x.dtype), x[..., ROPE_DIM:]], axis=-1)


def mixer_inputs(x, pos, norm_w, w_in, q_norm_w, k_norm_w):
    n, s, _ = x.shape
    z = rmsnorm(x, norm_w) @ w_in
    o_u = 3 * ATTN_WIDTH
    o_ga = o_u + SSM_WIDTH
    o_gb = o_ga + D_MODEL
    def heads(t):
        return t.reshape(n, s, N_ATTN_HEADS, HEAD_DIM)
    q = partial_rope(rmsnorm(heads(z[..., :ATTN_WIDTH]), q_norm_w), pos)
    k = partial_rope(rmsnorm(heads(z[..., ATTN_WIDTH:2 * ATTN_WIDTH]), k_norm_w), pos)
    v = heads(z[..., 2 * ATTN_WIDTH:o_u])
    return q, k, v, z[..., o_u:o_ga], z[..., o_ga:o_gb], z[..., o_gb:]


def softmax_stats(s):
    m = jnp.max(s, axis=-1, keepdims=True)
    lse = m + jnp.log(jnp.sum(jnp.exp(s - m), axis=-1, keepdims=True))
    return jnp.exp(s - lse), lse[..., 0]


def band_attention(q, k, v, n_steps):
    n, m, h, dh = q.shape
    nb = m // BLK
    def blocks(t):
        return t.astype(jnp.float32).reshape(n, nb, BLK, h, dh)
    qb, kb, vb = blocks(q), blocks(k), blocks(v)
    def with_prev(t):
        prev = jnp.pad(t, ((0, 0), (1, 0), (0, 0), (0, 0), (0, 0)))[:, :-1]
        return jnp.concatenate([prev, t], axis=2)
    kc, vc = with_prev(kb), with_prev(vb)
    s = jnp.einsum('nbqhd,nbkhd->nbhqk', qb, kc) / math.sqrt(dh)
    qi = jnp.arange(BLK)[:, None]
    kj = jnp.arange(2 * BLK)[None, :]
    dist = qi + BLK - kj
    kabs = jnp.arange(nb)[:, None, None] * BLK + kj[None] - BLK
    valid = (dist >= 0)[None] & (dist <= n_steps)[None] & (kabs >= 0)
    s = jnp.where(valid[None, :, None], s, NEG)
    p, lse = softmax_stats(s)
    o = jnp.einsum('nbhqk,nbkhd->nbqhd', p, vc)
    return o.reshape(n, m, h, dh), jnp.swapaxes(lse, 2, 3).reshape(n, m, h)


def dilated_attention_prompt(q, k, v, window, dil):
    b, s, h, dh = q.shape
    span = dil * BLK
    s_pad = -(-s // span) * span
    m = s_pad // dil
    def to_residues(t):
        t = jnp.pad(t, ((0, 0), (0, s_pad - s), (0, 0), (0, 0)))
        return t.reshape(b, m, dil, h, dh).transpose(0, 2, 1, 3, 4).reshape(b * dil, m, h, dh)
    o, lse = band_attention(to_residues(q), to_residues(k), to_residues(v), window // dil)
    o = o.reshape(b, dil, m, h, dh).transpose(0, 2, 1, 3, 4).reshape(b, s_pad, h, dh)[:, :s]
    lse = lse.reshape(b, dil, m, h).transpose(0, 2, 1, 3).reshape(b, s_pad, h)[:, :s]
    return o, lse


def dilated_attention_sample(q, k_full, v_full, window, dil):
    n, t, h, dh = q.shape
    past = k_full.shape[1] - t
    n_steps = window // dil
    idx = past + jnp.arange(t)[:, None] - dil * jnp.arange(n_steps + 1)[None, :]
    valid = idx >= 0
    flat = jnp.maximum(idx, 0).reshape(-1)
    kg = jnp.take(k_full, flat, axis=1).reshape(n, t, n_steps + 1, h, dh).astype(jnp.float32)
    vg = jnp.take(v_full, flat, axis=1).reshape(n, t, n_steps + 1, h, dh).astype(jnp.float32)
    s = jnp.einsum('nthd,ntjhd->nthj', q.astype(jnp.float32), kg) / math.sqrt(dh)
    s = jnp.where(valid[None, :, None, :], s, NEG)
    p, lse = softmax_stats(s)
    return jnp.einsum('nthj,ntjhd->nthd', p, vg), lse


def mix_dilations(outs, lses):
    w = jax.nn.softmax(jnp.stack(lses, 0), axis=0)
    return jnp.sum(w[..., None] * jnp.stack(outs, 0), axis=0)


def complex_affine_combine(e1, e2):
    a1r, a1i, b1r, b1i = e1
    a2r, a2i, b2r, b2i = e2
    return (a2r * a1r - a2i * a1i,
            a2r * a1i + a2i * a1r,
            a2r * b1r - a2i * b1i + b2r,
            a2r * b1i + a2i * b1r + b2i)


def s5_branch(u, x0, a_re, a_im, log_dt, b_re, b_im, c_re, c_im, d_skip, w_glu):
    n, s, _ = u.shape
    f32 = jnp.float32
    ug = u.astype(f32).reshape(n, s, SSM_GROUPS, SSM_CH)
    ar, ai = a_re.astype(f32), a_im.astype(f32)
    dt = jnp.exp(log_dt.astype(f32))[:, None]
    mag = jnp.exp(dt * ar)
    lam_re, lam_im = mag * jnp.cos(dt * ai), mag * jnp.sin(dt * ai)
    den = ar * ar + ai * ai
    nr = lam_re - 1.0
    coef_re = (nr * ar + lam_im * ai) / den
    coef_im = (lam_im * ar - nr * ai) / den
    br, bi = b_re.astype(f32), b_im.astype(f32)
    bb_re = coef_re[..., None] * br - coef_im[..., None] * bi
    bb_im = coef_re[..., None] * bi + coef_im[..., None] * br
    bu_re = jnp.einsum('nsgc,gpc->nsgp', ug, bb_re)
    bu_im = jnp.einsum('nsgc,gpc->nsgp', ug, bb_im)
    x0r, x0i = x0[..., 0].astype(f32), x0[..., 1].astype(f32)
    bu_re = bu_re.at[:, 0].add(lam_re * x0r - lam_im * x0i)
    bu_im = bu_im.at[:, 0].add(lam_re * x0i + lam_im * x0r)
    elems = (jnp.broadcast_to(lam_re, bu_re.shape), jnp.broadcast_to(lam_im, bu_re.shape), bu_re, bu_im)
    _, _, xr, xi = lax.associative_scan(complex_affine_combine, elems, axis=1)
    y = (jnp.einsum('nsgp,gcp->nsgc', xr, c_re.astype(f32))
         - jnp.einsum('nsgp,gcp->nsgc', xi, c_im.astype(f32))
         + d_skip.astype(f32).reshape(SSM_GROUPS, SSM_CH) * ug)
    y = jax.nn.gelu(y).reshape(n, s, SSM_WIDTH).astype(u.dtype)
    z = y @ w_glu
    out = z[..., :D_MODEL] * jax.nn.sigmoid(z[..., D_MODEL:])
    new_state = jnp.stack([xr[:, -1], xi[:, -1]], axis=-1)
    return out, new_state


def merge_branches(x, y_attn, y_ssm, g_a, g_b, w_attn_out, w_out):
    n, s = x.shape[:2]
    ya = y_attn.reshape(n, s, ATTN_OUT_WIDTH).astype(x.dtype) @ w_attn_out
    mixed = jax.nn.sigmoid(g_a) * ya + jax.nn.sigmoid(g_b) * y_ssm
    return x + mixed @ w_out


def peer_ffn(x, norm_w, w_query, sub_keys1, sub_keys2, expert_u, expert_v):
    n, s, d = x.shape
    ntok = n * s
    nblk = -(-ntok // PEER_BLOCK)
    h = rmsnorm(x, norm_w).reshape(ntok, d)
    h = jnp.pad(h, ((0, nblk * PEER_BLOCK - ntok), (0, 0))).reshape(nblk, PEER_BLOCK, d)
    k1 = sub_keys1.astype(jnp.float32)
    k2 = sub_keys2.astype(jnp.float32)
    def block(hx):
        q = unit_rms((hx @ w_query).reshape(PEER_BLOCK, PEER_HEADS, PEER_QDIM))
        s1 = jnp.einsum('thc,kc->thk', q[..., :PEER_HALF], k1)
        s2 = jnp.einsum('thc,kc->thk', q[..., PEER_HALF:], k2)
        v1, i1 = lax.top_k(s1, PEER_TOPK)
        v2, i2 = lax.top_k(s2, PEER_TOPK)
        cand = (v1[..., :, None] + v2[..., None, :]).reshape(PEER_BLOCK, PEER_HEADS, PEER_TOPK * PEER_TOPK)
        sc, ci = lax.top_k(cand, PEER_TOPK)
        e = (jnp.take_along_axis(i1, ci // PEER_TOPK, axis=-1) * PEER_NKEYS
             + jnp.take_along_axis(i2, ci % PEER_TOPK, axis=-1))
        g = jax.nn.softmax(sc, axis=-1)
        act = jax.nn.gelu(jnp.einsum('td,thkd->thk', hx, expert_u[e]).astype(jnp.float32)) * g
        return jnp.einsum('thk,thkd->td', act.astype(hx.dtype), expert_v[e])
    out = lax.map(block, h).reshape(nblk * PEER_BLOCK, d)[:ntok].reshape(n, s, d)
    return x + out


def setup_inputs(seed: int = 0) -> dict:
    key = jax.random.key(seed)
    ks = jax.random.split(key, 27)
    f32 = jnp.float32
    def nrm(i, shape, scale):
        return jax.random.normal(ks[i], shape, f32) * scale
    rows = [min(w, PAST_LEN) for w, _ in ATTN_CONFIGS]
    win_shape = lambda r: (DEPTH, DEC_BATCH, r, 2, HEADS_PER_GROUP, HEAD_DIM)
    return {
        'x_prompt': nrm(0, (BATCH, SEQ, D_MODEL), 1.0),
        'x_sample': nrm(1, (DEC_BATCH, DEC_SEQ, D_MODEL), 1.0),
        'cache_win1': nrm(2, win_shape(rows[0]), 1.0),
        'cache_win2': nrm(3, win_shape(rows[1]), 1.0),
        'cache_win3': nrm(4, win_shape(rows[2]), 1.0),
        'state_ssm': nrm(5, (DEPTH, DEC_BATCH, SSM_GROUPS, SSM_STATE, 2), 0.1),
        'norm_mix': 1.0 + nrm(6, (DEPTH, D_MODEL), 0.02),
        'w_in': nrm(7, (DEPTH, D_MODEL, IN_WIDTH), D_MODEL ** -0.5),
        'q_norm': 1.0 + nrm(8, (DEPTH, HEAD_DIM), 0.02),
        'k_norm': 1.0 + nrm(9, (DEPTH, HEAD_DIM), 0.02),
        'w_attn_out': nrm(10, (DEPTH, ATTN_OUT_WIDTH, D_MODEL), ATTN_OUT_WIDTH ** -0.5),
        'ssm_a_re': -0.5 + nrm(11, (DEPTH, SSM_GROUPS, SSM_STATE), 0.01),
        'ssm_a_im': jnp.pi * jnp.arange(SSM_STATE, dtype=f32)[None, None, :] + nrm(12, (DEPTH, SSM_GROUPS, SSM_STATE), 0.01),
        'ssm_log_dt': jax.random.uniform(ks[13], (DEPTH, SSM_GROUPS), f32, math.log(1e-3), math.log(1e-1)),
        'ssm_b_re': nrm(14, (DEPTH, SSM_GROUPS, SSM_STATE, SSM_CH), (2 * SSM_CH) ** -0.5),
        'ssm_b_im': nrm(15, (DEPTH, SSM_GROUPS, SSM_STATE, SSM_CH), (2 * SSM_CH) ** -0.5),
        'ssm_c_re': nrm(16, (DEPTH, SSM_GROUPS, SSM_CH, SSM_STATE), SSM_STATE ** -0.5),
        'ssm_c_im': nrm(17, (DEPTH, SSM_GROUPS, SSM_CH, SSM_STATE), SSM_STATE ** -0.5),
        'ssm_d': nrm(18, (DEPTH, SSM_WIDTH), 1.0),
        'w_glu': nrm(19, (DEPTH, SSM_WIDTH, 2 * D_MODEL), SSM_WIDTH ** -0.5),
        'w_out': nrm(20, (DEPTH, D_MODEL, D_MODEL), D_MODEL ** -0.5),
        'norm_ffn': 1.0 + nrm(21, (DEPTH, D_MODEL), 0.02),
        'peer_w_query': nrm(22, (DEPTH, D_MODEL, PEER_HEADS * PEER_QDIM), D_MODEL ** -0.5),
        'peer_sub_keys1': nrm(23, (DEPTH, PEER_NKEYS, PEER_HALF), PEER_HALF ** -0.5),
        'peer_sub_keys2': nrm(24, (DEPTH, PEER_NKEYS, PEER_HALF), PEER_HALF ** -0.5),
        'peer_u': nrm(25, (DEPTH, PEER_EXPERTS, D_MODEL), D_MODEL ** -0.5),
        'peer_v': nrm(26, (DEPTH, PEER_EXPERTS, D_MODEL), 0.25),
    }


def reference(x_prompt, x_sample, cache_win1, cache_win2, cache_win3, state_ssm, norm_mix, w_in, q_norm, k_norm, w_attn_out, ssm_a_re, ssm_a_im, ssm_log_dt, ssm_b_re, ssm_b_im, ssm_c_re, ssm_c_im, ssm_d, w_glu, w_out, norm_ffn, peer_w_query, peer_sub_keys1, peer_sub_keys2, peer_u, peer_v):
    caches = (cache_win1, cache_win2, cache_win3)
    b, s = x_prompt.shape[:2]
    t = x_sample.shape[1]
    pos_p = jnp.arange(s)
    pos_s = PAST_LEN + jnp.arange(t)
    xp, xs = x_prompt, x_sample
    win_p = ([], [], [])
    win_s = ([], [], [])
    ssm_p, ssm_s = [], []
    for l in range(DEPTH):
        ssm_w = (ssm_a_re[l], ssm_a_im[l], ssm_log_dt[l], ssm_b_re[l], ssm_b_im[l],
                 ssm_c_re[l], ssm_c_im[l], ssm_d[l], w_glu[l])
        peer_w = (norm_ffn[l], peer_w_query[l], peer_sub_keys1[l], peer_sub_keys2[l], peer_u[l], peer_v[l])
        q, k, v, u, g_a, g_b = mixer_inputs(xp, pos_p, norm_mix[l], w_in[l], q_norm[l], k_norm[l])
        outs, lses = [], []
        for g, (window, dil) in enumerate(ATTN_CONFIGS):
            hs = slice(g * HEADS_PER_GROUP, (g + 1) * HEADS_PER_GROUP)
            o, lse = dilated_attention_prompt(q[:, :, hs], k[:, :, hs], v[:, :, hs], window, dil)
            outs.append(o)
            lses.append(lse)
            rows = min(window, s)
            win_p[g].append(jnp.stack([k[:, s - rows:, hs], v[:, s - rows:, hs]], axis=2))
        y_ssm, st = s5_branch(u, jnp.zeros((b, SSM_GROUPS, SSM_STATE, 2), jnp.float32), *ssm_w)
        ssm_p.append(st)
        xp = merge_branches(xp, mix_dilations(outs, lses), y_ssm, g_a, g_b, w_attn_out[l], w_out[l])
        xp = peer_ffn(xp, *peer_w)
        q, k, v, u, g_a, g_b = mixer_inputs(xs, pos_s, norm_mix[l], w_in[l], q_norm[l], k_norm[l])
        outs, lses = [], []
        for g, (window, dil) in enumerate(ATTN_CONFIGS):
            hs = slice(g * HEADS_PER_GROUP, (g + 1) * HEADS_PER_GROUP)
            cache = caches[g][l]
            rows = cache.shape[1]
            k_full = jnp.concatenate([cache[:, :, 0], k[:, :, hs]], axis=1)
            v_full = jnp.concatenate([cache[:, :, 1], v[:, :, hs]], axis=1)
            o, lse = dilated_attention_sample(q[:, :, hs], k_full, v_full, window, dil)
            outs.append(o)
            lses.append(lse)
            win_s[g].append(jnp.stack([k_full[:, -rows:], v_full[:, -rows:]], axis=2))
        y_ssm, st = s5_branch(u, state_ssm[l], *ssm_w)
        ssm_s.append(st)
        xs = merge_branches(xs, mix_dilations(outs, lses), y_ssm, g_a, g_b, w_attn_out[l], w_out[l])
        xs = peer_ffn(xs, *peer_w)
    y_prompt = xp
    y_sample = xs
    win1_prompt = jnp.stack(win_p[0], axis=0)
    win2_prompt = jnp.stack(win_p[1], axis=0)
    win3_prompt = jnp.stack(win_p[2], axis=0)
    ssm_prompt = jnp.stack(ssm_p, axis=0)
    win1_sample = jnp.stack(win_s[0], axis=0)
    win2_sample = jnp.stack(win_s[1], axis=0)
    win3_sample = jnp.stack(win_s[2], axis=0)
    ssm_sample = jnp.stack(ssm_s, axis=0)
    return (y_prompt, y_sample, win1_prompt, win2_prompt, win3_prompt, ssm_prompt, win1_sample, win2_sample, win3_sample, ssm_sample)
```

```python
import functools
import math

import jax
import jax.numpy as jnp
import numpy as np
from jax import lax
from jax.experimental import pallas as pl
from jax.experimental.pallas import tpu as pltpu

F32 = jnp.float32
BF16 = jnp.bfloat16

D_MODEL = 1024
HEAD_DIM = 64
HEADS_PER_GROUP = 4
GROUP_WIDTH = HEADS_PER_GROUP * HEAD_DIM
ATTN_CONFIGS = ((128, 1), (512, 4), (2048, 16))
N_GROUPS = len(ATTN_CONFIGS)
ATTN_WIDTH = N_GROUPS * GROUP_WIDTH
ROPE_DIM = HEAD_DIM // 4
ROPE_HALF = ROPE_DIM // 2
ROPE_THETA = 500000.0
PAST_LEN = 2048
BAND = 128
SSM_CH = 16
SSM_GROUPS = 32
SSM_WIDTH = SSM_CH * SSM_GROUPS
SSM_STATE = 64
SSM_FLAT = SSM_GROUPS * SSM_STATE
PEER_HEADS = 8
PEER_NKEYS = 128
PEER_EXPERTS = PEER_NKEYS * PEER_NKEYS
PEER_QDIM = 256
PEER_HALF = PEER_QDIM // 2
PEER_TOPK = 16
EPS = 1e-6
NEG = -1e30

LANES = 128
SUBLANES = 8
VMEM_LIMIT_BYTES = 56 * 1024 * 1024

COL_BLOCK = 256
Z_Q, Z_K, Z_V = 0, ATTN_WIDTH, 2 * ATTN_WIDTH
Z_U = 2560
Z_GA = 3072
Z_GB = 4096
Z_WIDTH = 5120
IN_WIDTH = 3 * ATTN_WIDTH + SSM_WIDTH + 2 * D_MODEL
N_IN_BLOCKS = IN_WIDTH // COL_BLOCK
N_QK_BLOCKS = 2 * ATTN_WIDTH // COL_BLOCK
PAD_BLOCK = 3 * ATTN_WIDTH // COL_BLOCK

NT_DIMS = (((1,), (1,)), ((), ()))


def _params(*semantics):
    return pltpu.CompilerParams(dimension_semantics=semantics, vmem_limit_bytes=VMEM_LIMIT_BYTES)


def _inproj_kernel(x_ref, nw_ref, w_ref, hw_ref, cos_ref, sa_ref, sb_ref, seg_ref, z_ref, xn_ref):
    j = pl.program_id(1)

    @pl.when(j == 0)
    def _():
        x = x_ref[...]
        ms = jnp.mean(x * x, axis=-1, keepdims=True)
        xn_ref[...] = (x * lax.rsqrt(ms + EPS) * nw_ref[...]).astype(BF16)

    z = jnp.dot(xn_ref[...], w_ref[...], preferred_element_type=F32)

    @pl.when(j < N_QK_BLOCKS)
    def _():
        zz = z * z
        hi = zz.astype(BF16)
        lo = (zz - hi.astype(F32)).astype(BF16)
        ms = jnp.dot(hi, seg_ref[...], preferred_element_type=F32) + jnp.dot(
            lo, seg_ref[...], preferred_element_type=F32
        )
        y = z * lax.rsqrt(ms + EPS) * hw_ref[...]
        cos = jnp.concatenate([cos_ref[...], cos_ref[...]], axis=1)
        sa = jnp.concatenate([sa_ref[...], sa_ref[...]], axis=1)
        sb = jnp.concatenate([sb_ref[...], sb_ref[...]], axis=1)
        z_ref[...] = (
            y * cos
            + pltpu.roll(y, COL_BLOCK - ROPE_HALF, axis=1) * sa
            + pltpu.roll(y, ROPE_HALF, axis=1) * sb
        )

    @pl.when(j >= N_QK_BLOCKS)
    def _():
        z_ref[...] = z


def _rope_tables(pos):
    inv = ROPE_THETA ** (-jnp.arange(ROPE_HALF, dtype=F32) / ROPE_HALF)
    ang = pos.astype(F32)[:, None] * inv[None, :]
    cos8, sin8 = jnp.cos(ang), jnp.sin(ang)
    t = pos.shape[0]
    ones = jnp.ones((t, HEAD_DIM - ROPE_DIM), F32)
    zeros = jnp.zeros((t, HEAD_DIM - ROPE_DIM), F32)
    z8 = jnp.zeros((t, ROPE_HALF), F32)
    cos = jnp.concatenate([cos8, cos8, ones], axis=1)
    sa = jnp.concatenate([-sin8, z8, zeros], axis=1)
    sb = jnp.concatenate([z8, sin8, zeros], axis=1)
    rep = LANES // HEAD_DIM
    return jnp.tile(cos, (1, rep)), jnp.tile(sa, (1, rep)), jnp.tile(sb, (1, rep))


def _segment_mean_matrix():
    seg = np.arange(COL_BLOCK) // HEAD_DIM
    return jnp.asarray((seg[:, None] == seg[None, :]) / HEAD_DIM, dtype=BF16)


def _inproj(x, pos, norm_w, w_in_bf, q_norm, k_norm, tm):
    t = x.shape[0]
    cos, sa, sb = _rope_tables(pos)
    hw = jnp.concatenate(
        [jnp.tile(q_norm, ATTN_WIDTH // HEAD_DIM), jnp.tile(k_norm, ATTN_WIDTH // HEAD_DIM)]
    ).reshape(1, 2 * ATTN_WIDTH)
    row = lambda i, j: (i, 0)
    return pl.pallas_call(
        _inproj_kernel,
        grid=(t // tm, N_IN_BLOCKS),
        in_specs=[
            pl.BlockSpec((tm, D_MODEL), row),
            pl.BlockSpec((1, D_MODEL), lambda i, j: (0, 0)),
            pl.BlockSpec((D_MODEL, COL_BLOCK), lambda i, j: (0, j)),
            pl.BlockSpec((1, COL_BLOCK), lambda i, j: (0, jnp.minimum(j, N_QK_BLOCKS - 1))),
            pl.BlockSpec((tm, LANES), row),
            pl.BlockSpec((tm, LANES), row),
            pl.BlockSpec((tm, LANES), row),
            pl.BlockSpec((COL_BLOCK, COL_BLOCK), lambda i, j: (0, 0)),
        ],
        out_specs=pl.BlockSpec((tm, COL_BLOCK), lambda i, j: (i, j + (j >= PAD_BLOCK).astype(jnp.int32))),
        out_shape=jax.ShapeDtypeStruct((t, Z_WIDTH), F32),
        scratch_shapes=[pltpu.VMEM((tm, D_MODEL), BF16)],
        compiler_params=_params("parallel", "arbitrary"),
        name="inproj",
    )(x, norm_w.reshape(1, D_MODEL), w_in_bf, hw, cos, sa, sb, _segment_mean_matrix())


def _band_kernel(q_ref, kp_ref, kc_ref, vp_ref, vc_ref, o_ref, l_ref):
    mb = pl.program_id(2)
    qi = lax.broadcasted_iota(jnp.int32, (BAND, 2 * BAND), 0)
    kj = lax.broadcasted_iota(jnp.int32, (BAND, 2 * BAND), 1)
    dist = qi + BAND - kj
    valid = (dist >= 0) & (dist <= BAND) & ((kj >= BAND) | (mb > 0))
    q = q_ref[...].astype(BF16)
    k = jnp.concatenate([kp_ref[...], kc_ref[...]], axis=0).astype(BF16)
    v = jnp.concatenate([vp_ref[...], vc_ref[...]], axis=0).astype(BF16)
    outs, lses = [], []
    for h in range(HEADS_PER_GROUP):
        sl = slice(h * HEAD_DIM, (h + 1) * HEAD_DIM)
        s = lax.dot_general(q[:, sl], k[:, sl], NT_DIMS, preferred_element_type=F32) / math.sqrt(HEAD_DIM)
        s = jnp.where(valid, s, NEG)
        m = jnp.max(s, axis=-1, keepdims=True)
        lse = m + jnp.log(jnp.sum(jnp.exp(s - m), axis=-1, keepdims=True))
        p = jnp.exp(s - lse)
        outs.append(jnp.dot(p.astype(BF16), v[:, sl], preferred_element_type=F32))
        lses.append(jnp.broadcast_to(lse, (BAND, HEAD_DIM)))
    o_ref[...] = jnp.concatenate(outs, axis=1)
    l_ref[...] = jnp.concatenate(lses, axis=1)


def _band_attention(z, batch, seq, group, dil):
    nblk = seq // (dil * BAND)
    zv = z.reshape(batch, nblk, BAND, dil * Z_WIDTH)
    zcols = Z_WIDTH // COL_BLOCK

    def col(section):
        base = section // COL_BLOCK + group
        return base

    def spec(section, prev):
        base = col(section)
        if prev:
            return pl.BlockSpec(
                (None, None, BAND, COL_BLOCK), lambda b, r, m: (b, jnp.maximum(m - 1, 0), 0, r * zcols + base)
            )
        return pl.BlockSpec((None, None, BAND, COL_BLOCK), lambda b, r, m: (b, m, 0, r * zcols + base))

    out_spec = pl.BlockSpec((None, None, BAND, COL_BLOCK), lambda b, r, m: (b, m, 0, r))
    shape = jax.ShapeDtypeStruct((batch, nblk, BAND, dil * GROUP_WIDTH), F32)
    o, lse = pl.pallas_call(
        _band_kernel,
        grid=(batch, dil, nblk),
        in_specs=[spec(Z_Q, False), spec(Z_K, True), spec(Z_K, False), spec(Z_V, True), spec(Z_V, False)],
        out_specs=[out_spec, out_spec],
        out_shape=[shape, shape],
        compiler_params=_params("parallel", "parallel", "arbitrary"),
        name=f"band_attn_d{dil}",
    )(zv, zv, zv, zv, zv)
    return o.reshape(batch * seq, GROUP_WIDTH), lse.reshape(batch * seq, GROUP_WIDTH)


KV_WIDTH = 2 * GROUP_WIDTH
WINDOW_BLOCK_BYTES = 4 * 1024 * 1024


def _window_kernel(dil, q_ref, new_ref, c_ref, *refs):
    out_ref, o_ref, l_ref = refs[-3:]
    nb, _, rows = c_ref.shape
    scale = 1.0 / math.sqrt(HEAD_DIM)
    row_id = lax.broadcasted_iota(jnp.int32, (HEADS_PER_GROUP, rows), 1)
    live = (row_id & (dil - 1)) == 0
    is_last = lax.broadcasted_iota(jnp.int32, (KV_WIDTH, rows), 1) == rows - 1
    per_head = lambda a: a.reshape(HEADS_PER_GROUP, HEAD_DIM, a.shape[-1])
    for i in range(nb):
        x = c_ref[i]
        q = q_ref[i]
        new = new_ref[i]
        k, v = x[:GROUP_WIDTH], x[GROUP_WIDTH:]
        s = jnp.where(live, jnp.sum(per_head(k * q), axis=1) * scale, NEG)
        s_new = jnp.sum(per_head(new[:GROUP_WIDTH] * q), axis=1) * scale
        m = jnp.maximum(jnp.max(s, axis=-1, keepdims=True), s_new)
        lse = m + jnp.log(jnp.sum(jnp.exp(s - m), axis=-1, keepdims=True) + jnp.exp(s_new - m))
        p = jnp.exp(s - lse)
        p_new = jnp.exp(s_new - lse)
        spread = lambda a: jnp.broadcast_to(a[:, None, :], (HEADS_PER_GROUP, HEAD_DIM, a.shape[-1])).reshape(
            GROUP_WIDTH, a.shape[-1]
        )
        o_ref[i] = jnp.sum(v * spread(p), axis=-1, keepdims=True) + spread(p_new) * new[GROUP_WIDTH:]
        l_ref[i] = lse
        out_ref[i] = jnp.where(is_last, new, pltpu.roll(x, rows - 1, axis=1))


def _window_update(win, cache_t, layer, z, group, dil):
    depth, n, _, rows = cache_t.shape
    assert rows == dil * BAND, "window buffer shorter than the window is not supported"
    k_new, v_new = _kv_rows(z, group)
    q = z[:, Z_Q + group * GROUP_WIDTH : Z_Q + (group + 1) * GROUP_WIDTH].reshape(n, GROUP_WIDTH, 1)
    new = jnp.concatenate([k_new, v_new], axis=-1).reshape(n, KV_WIDTH, 1)
    nb = max(1, min(n, WINDOW_BLOCK_BYTES // (KV_WIDTH * rows * 4)))
    col = lambda width: pl.BlockSpec((nb, width, 1), lambda i: (i, 0, 0))
    buf = pl.BlockSpec((None, nb, KV_WIDTH, rows), lambda i: (layer, i, 0, 0))
    in_specs = [col(GROUP_WIDTH), col(KV_WIDTH), buf]
    args = [q, new, cache_t]
    aliases = {}
    if win is not None:
        in_specs.append(pl.BlockSpec(memory_space=pl.ANY))
        args.append(win)
        aliases = {3: 0}
    win, o, lse = pl.pallas_call(
        functools.partial(_window_kernel, dil),
        grid=(n // nb,),
        in_specs=in_specs,
        out_specs=[buf, col(GROUP_WIDTH), col(HEADS_PER_GROUP)],
        out_shape=[
            jax.ShapeDtypeStruct(cache_t.shape, F32),
            jax.ShapeDtypeStruct((n, GROUP_WIDTH, 1), F32),
            jax.ShapeDtypeStruct((n, HEADS_PER_GROUP, 1), F32),
        ],
        input_output_aliases=aliases,
        compiler_params=_params("parallel"),
        name=f"window_d{dil}",
    )(*args)
    lse = jnp.broadcast_to(lse, (n, HEADS_PER_GROUP, HEAD_DIM)).reshape(n, GROUP_WIDTH)
    return win, (o.reshape(n, GROUP_WIDTH), lse)


SSM_TILES = SSM_WIDTH // LANES
TILE_STATES = SSM_FLAT // SSM_TILES


def _s5_discretize(ar_ref, ai_ref, ldt_ref, bre_ref, bim_ref, lam_re_ref, lam_im_ref, bbre_ref, bbim_ref):
    ar, ai = ar_ref[...], ai_ref[...]
    dt = jnp.exp(ldt_ref[...])
    mag = jnp.exp(dt * ar)
    lam_re, lam_im = mag * jnp.cos(dt * ai), mag * jnp.sin(dt * ai)
    den = ar * ar + ai * ai
    nr = lam_re - 1.0
    coef_re = (nr * ar + lam_im * ai) / den
    coef_im = (lam_im * ar - nr * ai) / den
    lam_re_ref[...] = jnp.broadcast_to(lam_re, lam_re_ref.shape)
    lam_im_ref[...] = jnp.broadcast_to(lam_im, lam_im_ref.shape)
    for j in range(SSM_TILES):
        sl = slice(j * TILE_STATES, (j + 1) * TILE_STATES)
        br, bi = bre_ref[j], bim_ref[j]
        bbre_ref[j] = (coef_re[:, sl] * br - coef_im[:, sl] * bi).astype(BF16)
        bbim_ref[j] = (coef_re[:, sl] * bi + coef_im[:, sl] * br).astype(BF16)


def _s5_input_map(u, bbre_ref, bbim_ref, xr_ref, xi_ref):
    ub = u.astype(BF16)
    for j in range(SSM_TILES):
        uj = ub[:, j * LANES : (j + 1) * LANES]
        sl = slice(j * TILE_STATES, (j + 1) * TILE_STATES)
        xr_ref[:, sl] = jnp.dot(uj, bbre_ref[j], preferred_element_type=F32)
        xi_ref[:, sl] = jnp.dot(uj, bbim_ref[j], preferred_element_type=F32)


def _s5_output_map(u, xr_ref, xi_ref, cre_ref, cim_ref, d_ref, y_ref):
    for j in range(SSM_TILES):
        sl = slice(j * TILE_STATES, (j + 1) * TILE_STATES)
        ch = slice(j * LANES, (j + 1) * LANES)
        y = (
            jnp.dot(xr_ref[:, sl].astype(BF16), cre_ref[j], preferred_element_type=F32)
            - jnp.dot(xi_ref[:, sl].astype(BF16), cim_ref[j], preferred_element_type=F32)
            + d_ref[:, ch] * u[:, ch]
        )
        y_ref[:, ch] = jax.nn.gelu(y)


SCAN_COLS = 256


def _cmul(ar, ai, br, bi):
    return ar * br - ai * bi, ar * bi + ai * br


def _s5_power_tables(lam_re_ref, lam_im_ref, pw_ref):
    l1 = (lam_re_ref[...], lam_im_ref[...])
    l2 = _cmul(*l1, *l1)
    l4 = _cmul(*l2, *l2)
    sub = lax.broadcasted_iota(jnp.int32, l1[0].shape, 0)
    q = l1
    for bit, p in ((1, l1), (2, l2), (4, l4)):
        nxt = _cmul(*q, *p)
        on = (sub & bit) != 0
        q = (jnp.where(on, nxt[0], q[0]), jnp.where(on, nxt[1], q[1]))
    tables = []
    for s, p in ((1, l1), (2, l2), (4, l4)):
        tables += [jnp.where(sub >= s, p[0], 0.0), jnp.where(sub >= s, p[1], 0.0)]
    tables += [q[0], q[1]]
    for k, t in enumerate(tables):
        pw_ref[k] = t


def _s5_scan_kernel(
    u_ref, ar_ref, ai_ref, ldt_ref, bre_ref, bim_ref, cre_ref, cim_ref, d_ref,
    y_ref, sr_ref, si_ref,
    lam_re_ref, lam_im_ref, bbre_ref, bbim_ref, xr_ref, xi_ref, cr_ref, ci_ref, pw_ref,
):
    c = pl.program_id(1)
    tm = u_ref.shape[0]

    @pl.when(c == 0)
    def _():
        _s5_discretize(ar_ref, ai_ref, ldt_ref, bre_ref, bim_ref, lam_re_ref, lam_im_ref, bbre_ref, bbim_ref)
        _s5_power_tables(lam_re_ref, lam_im_ref, pw_ref)
        cr_ref[...] = jnp.zeros_like(cr_ref)
        ci_ref[...] = jnp.zeros_like(ci_ref)

    u = u_ref[...]
    _s5_input_map(u, bbre_ref, bbim_ref, xr_ref, xi_ref)

    for cb in range(SSM_FLAT // SCAN_COLS):
        cols = slice(cb * SCAN_COLS, (cb + 1) * SCAN_COLS)
        tabs = [pw_ref[k, :, cols] for k in range(8)]

        def block(i, carry):
            rows = pl.ds(pl.multiple_of(i * SUBLANES, SUBLANES), SUBLANES)
            x = (xr_ref[rows, cols], xi_ref[rows, cols])
            for n, s in enumerate((1, 2, 4)):
                prev = (pltpu.roll(x[0], s, axis=0), pltpu.roll(x[1], s, axis=0))
                add = _cmul(tabs[2 * n], tabs[2 * n + 1], *prev)
                x = (x[0] + add[0], x[1] + add[1])
            add = _cmul(tabs[6], tabs[7], *carry)
            x = (x[0] + add[0], x[1] + add[1])
            xr_ref[rows, cols] = x[0]
            xi_ref[rows, cols] = x[1]
            last = slice(SUBLANES - 1, SUBLANES)
            return (jnp.broadcast_to(x[0][last], x[0].shape), jnp.broadcast_to(x[1][last], x[1].shape))

        carry = lax.fori_loop(0, tm // SUBLANES, block, (cr_ref[:, cols], ci_ref[:, cols]))
        cr_ref[:, cols] = carry[0]
        ci_ref[:, cols] = carry[1]

    _s5_output_map(u, xr_ref, xi_ref, cre_ref, cim_ref, d_ref, y_ref)

    @pl.when(c == pl.num_programs(1) - 1)
    def _():
        sr_ref[...] = cr_ref[0:1, :]
        si_ref[...] = ci_ref[0:1, :]


def _s5_step_kernel(
    u_ref, x0r_ref, x0i_ref, ar_ref, ai_ref, ldt_ref, bre_ref, bim_ref, cre_ref, cim_ref, d_ref,
    y_ref, sr_ref, si_ref,
    lam_re_ref, lam_im_ref, bbre_ref, bbim_ref, xr_ref, xi_ref,
):
    _s5_discretize(ar_ref, ai_ref, ldt_ref, bre_ref, bim_ref, lam_re_ref, lam_im_ref, bbre_ref, bbim_ref)
    u = u_ref[...]
    _s5_input_map(u, bbre_ref, bbim_ref, xr_ref, xi_ref)
    lr, li = lam_re_ref[0:1, :], lam_im_ref[0:1, :]
    x0r, x0i = x0r_ref[...], x0i_ref[...]
    xr_ref[...] = lr * x0r - li * x0i + xr_ref[...]
    xi_ref[...] = lr * x0i + li * x0r + xi_ref[...]
    _s5_output_map(u, xr_ref, xi_ref, cre_ref, cim_ref, d_ref, y_ref)
    sr_ref[...] = xr_ref[...]
    si_ref[...] = xi_ref[...]


def _s5_weights(a_re, a_im, log_dt, b_re, b_im, c_re, c_im, d_skip):
    gpt = SSM_GROUPS // SSM_TILES
    eye = jnp.eye(gpt, dtype=F32)

    def expand_b(b):
        b4 = b.astype(F32).reshape(SSM_TILES, gpt, SSM_STATE, SSM_CH)
        return jnp.einsum("jgpc,gh->jgchp", b4, eye).reshape(SSM_TILES, LANES, TILE_STATES)

    def expand_c(cm):
        c4 = cm.astype(F32).reshape(SSM_TILES, gpt, SSM_CH, SSM_STATE)
        return jnp.einsum("jgcp,gh->jhpgc", c4, eye).reshape(SSM_TILES, TILE_STATES, LANES).astype(BF16)

    flat = lambda a: a.astype(F32).reshape(1, SSM_FLAT)
    ldt = jnp.broadcast_to(log_dt.astype(F32)[:, None], (SSM_GROUPS, SSM_STATE)).reshape(1, SSM_FLAT)
    return (
        flat(a_re), flat(a_im), ldt, expand_b(b_re), expand_b(b_im), expand_c(c_re), expand_c(c_im),
        d_skip.astype(F32).reshape(1, SSM_WIDTH),
    )


def _s5_weight_specs(index_map):
    full = lambda shape: pl.BlockSpec(shape, lambda *a: (0,) * len(shape))
    return [
        full((1, SSM_FLAT)), full((1, SSM_FLAT)), full((1, SSM_FLAT)),
        full((SSM_TILES, LANES, TILE_STATES)), full((SSM_TILES, LANES, TILE_STATES)),
        full((SSM_TILES, TILE_STATES, LANES)), full((SSM_TILES, TILE_STATES, LANES)),
        full((1, SSM_WIDTH)),
    ]


def _s5_scratch(rows):
    return [
        pltpu.VMEM((SUBLANES, SSM_FLAT), F32), pltpu.VMEM((SUBLANES, SSM_FLAT), F32),
        pltpu.VMEM((SSM_TILES, LANES, TILE_STATES), BF16), pltpu.VMEM((SSM_TILES, LANES, TILE_STATES), BF16),
        pltpu.VMEM((rows, SSM_FLAT), F32), pltpu.VMEM((rows, SSM_FLAT), F32),
    ]


def _s5_prompt(z, batch, seq, weights, tm):
    nchunk = seq // tm
    ucol = Z_U // SSM_WIDTH
    state_spec = pl.BlockSpec((None, 1, SSM_FLAT), lambda b, c: (b, 0, 0))
    state_shape = jax.ShapeDtypeStruct((batch, 1, SSM_FLAT), F32)
    return pl.pallas_call(
        _s5_scan_kernel,
        grid=(batch, nchunk),
        in_specs=[pl.BlockSpec((tm, SSM_WIDTH), lambda b, c: (b * nchunk + c, ucol))] + _s5_weight_specs(None),
        out_specs=[pl.BlockSpec((tm, SSM_WIDTH), lambda b, c: (b * nchunk + c, 0)), state_spec, state_spec],
        out_shape=[jax.ShapeDtypeStruct((batch * seq, SSM_WIDTH), F32), state_shape, state_shape],
        scratch_shapes=_s5_scratch(tm)
        + [
            pltpu.VMEM((SUBLANES, SSM_FLAT), F32), pltpu.VMEM((SUBLANES, SSM_FLAT), F32),
            pltpu.VMEM((8, SUBLANES, SSM_FLAT), F32),
        ],
        compiler_params=_params("parallel", "arbitrary"),
        name="s5_scan",
    )(z, *weights)


def _s5_sample(z, x0r, x0i, weights):
    n = z.shape[0]
    ucol = Z_U // SSM_WIDTH
    st = pl.BlockSpec((n, SSM_FLAT), lambda i: (0, 0))
    return pl.pallas_call(
        _s5_step_kernel,
        grid=(1,),
        in_specs=[pl.BlockSpec((n, SSM_WIDTH), lambda i: (0, ucol)), st, st] + _s5_weight_specs(None),
        out_specs=[pl.BlockSpec((n, SSM_WIDTH), lambda i: (0, 0)), st, st],
        out_shape=[
            jax.ShapeDtypeStruct((n, SSM_WIDTH), F32),
            jax.ShapeDtypeStruct((n, SSM_FLAT), F32),
            jax.ShapeDtypeStruct((n, SSM_FLAT), F32),
        ],
        scratch_shapes=_s5_scratch(n),
        compiler_params=_params("arbitrary"),
        name="s5_step",
    )(z, x0r, x0i, *weights)


def _merge_kernel(
    x_ref, o1_ref, o2_ref, o3_ref, l1_ref, l2_ref, l3_ref, ys_ref, ga_ref, gb_ref,
    wao_ref, wglu_ref, wout_ref, nf_ref, h_ref, hn_ref,
):
    l1, l2, l3 = l1_ref[...], l2_ref[...], l3_ref[...]
    m = jnp.maximum(jnp.maximum(l1, l2), l3)
    e1, e2, e3 = jnp.exp(l1 - m), jnp.exp(l2 - m), jnp.exp(l3 - m)
    y_attn = (e1 * o1_ref[...] + e2 * o2_ref[...] + e3 * o3_ref[...]) / (e1 + e2 + e3)
    ya = jnp.dot(y_attn.astype(BF16), wao_ref[...], preferred_element_type=F32)
    zg = jnp.dot(ys_ref[...].astype(BF16), wglu_ref[...], preferred_element_type=F32)
    y_ssm = zg[:, :D_MODEL] * jax.nn.sigmoid(zg[:, D_MODEL:])
    mixed = jax.nn.sigmoid(ga_ref[...]) * ya + jax.nn.sigmoid(gb_ref[...]) * y_ssm
    h = x_ref[...] + jnp.dot(mixed.astype(BF16), wout_ref[...], preferred_element_type=F32)
    h_ref[...] = h
    ms = jnp.mean(h * h, axis=-1, keepdims=True)
    hn_ref[...] = (h * lax.rsqrt(ms + EPS) * nf_ref[...]).astype(BF16)


def _merge(x, attn, ys, z, wao_bf, wglu_bf, wout_bf, norm_ffn, tm):
    t = x.shape[0]
    (o1, l1), (o2, l2), (o3, l3) = attn
    row = lambda i: (i, 0)
    grp = pl.BlockSpec((tm, GROUP_WIDTH), row)
    full = lambda shape: pl.BlockSpec(shape, lambda i: (0, 0))
    return pl.pallas_call(
        _merge_kernel,
        grid=(t // tm,),
        in_specs=[
            pl.BlockSpec((tm, D_MODEL), row), grp, grp, grp, grp, grp, grp,
            pl.BlockSpec((tm, SSM_WIDTH), row),
            pl.BlockSpec((tm, D_MODEL), lambda i: (i, Z_GA // D_MODEL)),
            pl.BlockSpec((tm, D_MODEL), lambda i: (i, Z_GB // D_MODEL)),
            full((GROUP_WIDTH, D_MODEL)), full((SSM_WIDTH, 2 * D_MODEL)), full((D_MODEL, D_MODEL)),
            full((1, D_MODEL)),
        ],
        out_specs=[pl.BlockSpec((tm, D_MODEL), row), pl.BlockSpec((tm, D_MODEL), row)],
        out_shape=[jax.ShapeDtypeStruct((t, D_MODEL), F32), jax.ShapeDtypeStruct((t, D_MODEL), BF16)],
        compiler_params=_params("parallel"),
        name="merge",
    )(x, o1, o2, o3, l1, l2, l3, ys, z, z, wao_bf, wglu_bf, wout_bf, norm_ffn.reshape(1, D_MODEL))


def _merge_exchange_pairs(n):
    pairs = []
    t = max(1, math.ceil(math.log2(n)))
    p = 1 << (t - 1)
    while p > 0:
        q, r, d = 1 << (t - 1), 0, p
        while d > 0:
            pairs += [(i, i + d) for i in range(n - d) if (i & p) == r]
            d, q, r = q - p, q >> 1, p
        p >>= 1
    return pairs


SORT16 = tuple(_merge_exchange_pairs(PEER_TOPK))
BITONIC16 = tuple((i, i + d) for d in (8, 4, 2, 1) for i in range(PEER_TOPK) if not i & d)


def _exchange(v, pairs):
    v = list(v)
    for i, j in pairs:
        hi, lo = jnp.maximum(v[i], v[j]), jnp.minimum(v[i], v[j])
        v[i], v[j] = hi, lo
    return v


def _top16_desc(rows):
    v = _exchange(rows, SORT16)
    for shift in (4, 2, 1):
        other = [pltpu.roll(x, shift, axis=0) for x in v]
        v = [jnp.maximum(v[i], other[PEER_TOPK - 1 - i]) for i in range(PEER_TOPK)]
        v = _exchange(v, BITONIC16)
    return v


def _by_sublane(v, start):
    sub = lax.broadcasted_iota(jnp.int32, v[0].shape, 0)
    out = v[start]
    for s in range(1, SUBLANES):
        out = jnp.where(sub == s, v[start + s], out)
    return out


_CAND_ROWS = ((2, 5), (3, 4), (4, 3), (5, 2), (6, 2), (7, 2))


def _top16_of_sums(v1, v2):
    sub = lax.broadcasted_iota(jnp.int32, v1[0].shape, 0)
    v2_lo, v2_hi = _by_sublane(v2, 0), _by_sublane(v2, SUBLANES)
    neg = jnp.full(v1[0].shape, -jnp.inf, F32)
    cand = [v1[0] + v2_lo, v1[0] + v2_hi, v1[1] + v2_lo]
    for a, nb in _CAND_ROWS:
        cand.append(jnp.where(sub < nb, v1[a] + v2_lo, neg))
    cand.append(_by_sublane(v1, SUBLANES) + v2[0])
    cand += [neg] * (PEER_TOPK - len(cand))
    return _top16_desc(cand)


def _route_kernel(hn_ref, wq_ref, k1_ref, k2_ref, s1_ref, s2_ref, p2_ref, cf_ref, tau_ref):
    tm = hn_ref.shape[0]
    qt = lax.dot_general(wq_ref[...], hn_ref[...], NT_DIMS, preferred_element_type=F32)
    taus = []
    for h in range(PEER_HEADS):
        q = qt[h * PEER_QDIM : (h + 1) * PEER_QDIM, :]
        qn = (q * lax.rsqrt(jnp.mean(q * q, axis=0, keepdims=True) + EPS)).astype(BF16)
        s1 = jnp.dot(k1_ref[...], qn[:PEER_HALF], preferred_element_type=F32)
        s2 = jnp.dot(k2_ref[...], qn[PEER_HALF:], preferred_element_type=F32)
        rows = slice(h * PEER_NKEYS, (h + 1) * PEER_NKEYS)
        for g in range(tm // LANES):
            lanes = slice(g * LANES, (g + 1) * LANES)
            a, b = s1[:, lanes], s2[:, lanes]
            split = lambda x: [x[i * SUBLANES : (i + 1) * SUBLANES] for i in range(PEER_NKEYS // SUBLANES)]
            v1, v2 = _top16_desc(split(a)), _top16_desc(split(b))
            sc = _top16_of_sums(v1, v2)
            z = sc[0] * 0.0
            for k in range(PEER_TOPK):
                z = z + jnp.exp(sc[k] - sc[0])
            s1_ref[rows, lanes] = a
            s2_ref[rows, lanes] = b
            p2_ref[rows, lanes] = jnp.exp(b - v2[0][0:1])
            cf_ref[rows, lanes] = jnp.exp(a - v1[0][0:1]) / z[0:1]
            tau_ref[h : h + 1, lanes] = sc[PEER_TOPK - 1][0:1]


def _peer_route(hn, wq_t_bf, k1_bf, k2_bf, tm):
    t = hn.shape[0]
    score_spec = pl.BlockSpec((PEER_HEADS * PEER_NKEYS, tm), lambda i: (0, i))
    score_shape = jax.ShapeDtypeStruct((PEER_HEADS * PEER_NKEYS, t), F32)
    full = lambda shape: pl.BlockSpec(shape, lambda i: (0, 0))
    return pl.pallas_call(
        _route_kernel,
        grid=(t // tm,),
        in_specs=[
            pl.BlockSpec((tm, D_MODEL), lambda i: (i, 0)),
            full((PEER_HEADS * PEER_QDIM, D_MODEL)), full((PEER_NKEYS, PEER_HALF)), full((PEER_NKEYS, PEER_HALF)),
        ],
        out_specs=[score_spec] * 4 + [pl.BlockSpec((PEER_HEADS, tm), lambda i: (0, i))],
        out_shape=[score_shape] * 4 + [jax.ShapeDtypeStruct((PEER_HEADS, t), F32)],
        compiler_params=_params("parallel"),
        name="peer_route",
    )(hn, wq_t_bf, k1_bf, k2_bf)


EXPERT_ROWS = 8
EXPERT_TILE = EXPERT_ROWS * PEER_NKEYS


def _expert_kernel(hn_ref, h_ref, s1_ref, s2_ref, p2_ref, cf_ref, tau_ref, u_ref, vt_ref, o_ref, acc_ref, gt_ref):
    e = pl.program_id(1)
    tm = hn_ref.shape[0]

    @pl.when(e == 0)
    def _():
        acc_ref[...] = jnp.zeros_like(acc_ref)

    for g in range(tm // LANES):
        lanes = slice(g * LANES, (g + 1) * LANES)
        for rr in range(EXPERT_ROWS):
            gate = jnp.zeros((PEER_NKEYS, LANES), F32)
            for h in range(PEER_HEADS):
                keys = slice(h * PEER_NKEYS, (h + 1) * PEER_NKEYS)
                tile_rows = pl.ds(pl.multiple_of(h * PEER_NKEYS + e * EXPERT_ROWS, EXPERT_ROWS), EXPERT_ROWS)
                s1r = s1_ref[tile_rows, lanes][rr : rr + 1]
                cfr = cf_ref[tile_rows, lanes][rr : rr + 1]
                total = s2_ref[keys, lanes] + s1r
                gate = gate + jnp.where(total >= tau_ref[h : h + 1, lanes], p2_ref[keys, lanes] * cfr, 0.0)
            gt_ref[rr * PEER_NKEYS : (rr + 1) * PEER_NKEYS, lanes] = gate

    st = lax.dot_general(u_ref[...], hn_ref[...], NT_DIMS, preferred_element_type=F32)
    w = (jax.nn.gelu(st) * gt_ref[...]).astype(BF16)
    acc_ref[...] += jnp.dot(vt_ref[...], w, preferred_element_type=F32)

    @pl.when(e == pl.num_programs(1) - 1)
    def _():
        o_ref[...] = h_ref[...] + acc_ref[...].T


def _peer_experts(hn, h, routing, u_bf, vt_bf, tm):
    t = hn.shape[0]
    s1, s2, p2, cf, tau = routing
    tok = pl.BlockSpec((tm, D_MODEL), lambda i, e: (i, 0))
    score = pl.BlockSpec((PEER_HEADS * PEER_NKEYS, tm), lambda i, e: (0, i))
    return pl.pallas_call(
        _expert_kernel,
        grid=(t // tm, PEER_EXPERTS // EXPERT_TILE),
        in_specs=[
            tok, tok, score, score, score, score,
            pl.BlockSpec((PEER_HEADS, tm), lambda i, e: (0, i)),
            pl.BlockSpec((EXPERT_TILE, D_MODEL), lambda i, e: (e, 0)),
            pl.BlockSpec((D_MODEL, EXPERT_TILE), lambda i, e: (0, e)),
        ],
        out_specs=tok,
        out_shape=jax.ShapeDtypeStruct((t, D_MODEL), F32),
        scratch_shapes=[pltpu.VMEM((D_MODEL, tm), F32), pltpu.VMEM((EXPERT_TILE, tm), F32)],
        compiler_params=_params("parallel", "arbitrary"),
        name="peer_experts",
    )(hn, h, s1, s2, p2, cf, tau, u_bf, vt_bf)


def _kv_rows(z, group):
    k = z[..., Z_K + group * GROUP_WIDTH : Z_K + (group + 1) * GROUP_WIDTH]
    v = z[..., Z_V + group * GROUP_WIDTH : Z_V + (group + 1) * GROUP_WIDTH]
    return k, v


def _tile(t, want):
    return want if t % want == 0 else t


def kernel(x_prompt, x_sample, cache_win1, cache_win2, cache_win3, state_ssm, norm_mix, w_in, q_norm, k_norm, w_attn_out, ssm_a_re, ssm_a_im, ssm_log_dt, ssm_b_re, ssm_b_im, ssm_c_re, ssm_c_im, ssm_d, w_glu, w_out, norm_ffn, peer_w_query, peer_sub_keys1, peer_sub_keys2, peer_u, peer_v):
    caches = (cache_win1, cache_win2, cache_win3)
    batch, seq, _ = x_prompt.shape
    n_req, dec_seq, _ = x_sample.shape
    assert dec_seq == 1, "the sample group appends one token per request"
    depth = w_in.shape[0]
    tp, ts = batch * seq, n_req
    xp = x_prompt.reshape(tp, D_MODEL)
    xs = x_sample.reshape(ts, D_MODEL)
    pos_p = jnp.tile(jnp.arange(seq), batch)
    pos_s = jnp.full((ts,), PAST_LEN, jnp.int32)

    kv_shape = (2, HEADS_PER_GROUP, HEAD_DIM)
    caches_t = [
        jnp.transpose(c, (0, 1, 3, 4, 5, 2)).reshape(depth, n_req, KV_WIDTH, c.shape[2]) for c in caches
    ]
    win_s = [None] * N_GROUPS
    win_p = [[] for _ in ATTN_CONFIGS]
    ssm_p, ssm_s = [], []
    for l in range(depth):
        w_in_bf = w_in[l].astype(BF16)
        wao_bf, wglu_bf, wout_bf = w_attn_out[l].astype(BF16), w_glu[l].astype(BF16), w_out[l].astype(BF16)
        wq_t_bf = peer_w_query[l].astype(BF16).T
        k1_bf, k2_bf = peer_sub_keys1[l].astype(BF16), peer_sub_keys2[l].astype(BF16)
        u_bf, vt_bf = peer_u[l].astype(BF16), peer_v[l].astype(BF16).T
        s5w = _s5_weights(
            ssm_a_re[l], ssm_a_im[l], ssm_log_dt[l], ssm_b_re[l], ssm_b_im[l], ssm_c_re[l], ssm_c_im[l], ssm_d[l]
        )

        zp = _inproj(xp, pos_p, norm_mix[l], w_in_bf, q_norm[l], k_norm[l], _tile(tp, 512))
        attn_p = [_band_attention(zp, batch, seq, g, dil) for g, (_, dil) in enumerate(ATTN_CONFIGS)]
        ys_p, fr, fi = _s5_prompt(zp, batch, seq, s5w, _tile(seq, 512))
        hp, hnp = _merge(xp, attn_p, ys_p, zp, wao_bf, wglu_bf, wout_bf, norm_ffn[l], _tile(tp, 256))
        route_p = _peer_route(hnp, wq_t_bf, k1_bf, k2_bf, _tile(tp, 256))
        xp = _peer_experts(hnp, hp, route_p, u_bf, vt_bf, _tile(tp, 512))
        zp3 = zp.reshape(batch, seq, Z_WIDTH)
        for g, (window, _) in enumerate(ATTN_CONFIGS):
            rows = min(window, seq)
            k, v = _kv_rows(zp3[:, seq - rows :], g)
            shape = (batch, rows, HEADS_PER_GROUP, HEAD_DIM)
            win_p[g].append(jnp.stack([k.reshape(shape), v.reshape(shape)], axis=2))
        ssm_p.append(jnp.stack([fr, fi], axis=-1).reshape(batch, SSM_GROUPS, SSM_STATE, 2))

        zs = _inproj(xs, pos_s, norm_mix[l], w_in_bf, q_norm[l], k_norm[l], ts)
        attn_s = []
        for g, (_, dil) in enumerate(ATTN_CONFIGS):
            win_s[g], attn = _window_update(win_s[g], caches_t[g], l, zs, g, dil)
            attn_s.append(attn)
        st = state_ssm[l].astype(F32).reshape(ts, SSM_FLAT, 2)
        ys_s, nr, ni = _s5_sample(zs, st[..., 0], st[..., 1], s5w)
        hs, hns = _merge(xs, attn_s, ys_s, zs, wao_bf, wglu_bf, wout_bf, norm_ffn[l], ts)
        route_s = _peer_route(hns, wq_t_bf, k1_bf, k2_bf, ts)
        xs = _peer_experts(hns, hs, route_s, u_bf, vt_bf, ts)
        ssm_s.append(jnp.stack([nr, ni], axis=-1).reshape(ts, SSM_GROUPS, SSM_STATE, 2))

    win_s = [
        jnp.transpose(w.reshape(depth, n_req, *kv_shape, w.shape[-1]), (0, 1, 5, 2, 3, 4)) for w in win_s
    ]
    return (
        xp.reshape(batch, seq, D_MODEL),
        xs.reshape(n_req, 1, D_MODEL),
        jnp.stack(win_p[0], axis=0), jnp.stack(win_p[1], axis=0), jnp.stack(win_p[2], axis=0),
        jnp.stack(ssm_p, axis=0),
        win_s[0], win_s[1], win_s[2],
        jnp.stack(ssm_s, axis=0),
    )
```

```python
import functools
import math

import jax
import jax.numpy as jnp
import numpy as np
from jax import lax
from jax.experimental import pallas as pl
from jax.experimental.pallas import tpu as pltpu

F32 = jnp.float32
BF16 = jnp.bfloat16

D_MODEL = 1024
HEAD_DIM = 64
HEADS_PER_GROUP = 4
GROUP_WIDTH = HEADS_PER_GROUP * HEAD_DIM
ATTN_CONFIGS = ((128, 1), (512, 4), (2048, 16))
N_GROUPS = len(ATTN_CONFIGS)
ATTN_WIDTH = N_GROUPS * GROUP_WIDTH
ROPE_DIM = HEAD_DIM // 4
ROPE_HALF = ROPE_DIM // 2
ROPE_THETA = 500000.0
PAST_LEN = 2048
BAND = 128
SSM_CH = 16
SSM_GROUPS = 32
SSM_WIDTH = SSM_CH * SSM_GROUPS
SSM_STATE = 64
SSM_FLAT = SSM_GROUPS * SSM_STATE
PEER_HEADS = 8
PEER_NKEYS = 128
PEER_EXPERTS = PEER_NKEYS * PEER_NKEYS
PEER_QDIM = 256
PEER_HALF = PEER_QDIM // 2
PEER_TOPK = 16
EPS = 1e-6
NEG = -1e30

LANES = 128
SUBLANES = 8
VMEM_LIMIT_BYTES = 56 * 1024 * 1024

COL_BLOCK = 256
Z_Q, Z_K, Z_V = 0, ATTN_WIDTH, 2 * ATTN_WIDTH
Z_U = 2560
Z_GA = 3072
Z_GB = 4096
Z_WIDTH = 5120
IN_WIDTH = 3 * ATTN_WIDTH + SSM_WIDTH + 2 * D_MODEL
N_QK_BLOCKS = 2 * ATTN_WIDTH // COL_BLOCK
IN_PLAIN_SECTIONS = (
    (2 * ATTN_WIDTH, Z_V, ATTN_WIDTH),
    (3 * ATTN_WIDTH, Z_U, SSM_WIDTH),
    (3 * ATTN_WIDTH + SSM_WIDTH, Z_GA, D_MODEL),
    (3 * ATTN_WIDTH + SSM_WIDTH + D_MODEL, Z_GB, D_MODEL),
)

NT_DIMS = (((1,), (1,)), ((), ()))


def _params(*semantics):
    return pltpu.CompilerParams(dimension_semantics=semantics, vmem_limit_bytes=VMEM_LIMIT_BYTES)


def _inproj_kernel(x_ref, nw_ref, w_ref, hw_ref, cos_ref, sa_ref, sb_ref, seg_ref, z_ref):
    x = x_ref[...]
    ms = jnp.mean(x * x, axis=-1, keepdims=True)
    xn = (x * lax.rsqrt(ms + EPS) * nw_ref[...]).astype(BF16)
    cos = jnp.concatenate([cos_ref[...], cos_ref[...]], axis=1)
    sa = jnp.concatenate([sa_ref[...], sa_ref[...]], axis=1)
    sb = jnp.concatenate([sb_ref[...], sb_ref[...]], axis=1)
    for j in range(N_QK_BLOCKS):
        cols = slice(j * COL_BLOCK, (j + 1) * COL_BLOCK)
        z = jnp.dot(xn, w_ref[:, cols], preferred_element_type=F32)
        zz = z * z
        hi = zz.astype(BF16)
        lo = (zz - hi.astype(F32)).astype(BF16)
        ms = jnp.dot(hi, seg_ref[...], preferred_element_type=F32) + jnp.dot(
            lo, seg_ref[...], preferred_element_type=F32
        )
        y = z * lax.rsqrt(ms + EPS) * hw_ref[:, cols]
        z_ref[:, cols] = (
            y * cos
            + pltpu.roll(y, COL_BLOCK - ROPE_HALF, axis=1) * sa
            + pltpu.roll(y, ROPE_HALF, axis=1) * sb
        )
    for src, dst, width in IN_PLAIN_SECTIONS:
        z_ref[:, dst : dst + width] = jnp.dot(xn, w_ref[:, src : src + width], preferred_element_type=F32)
    z_ref[:, Z_V + ATTN_WIDTH : Z_U] = jnp.zeros((x.shape[0], Z_U - Z_V - ATTN_WIDTH), F32)


def _rope_tables(pos):
    inv = ROPE_THETA ** (-jnp.arange(ROPE_HALF, dtype=F32) / ROPE_HALF)
    ang = pos.astype(F32)[:, None] * inv[None, :]
    cos8, sin8 = jnp.cos(ang), jnp.sin(ang)
    t = pos.shape[0]
    ones = jnp.ones((t, HEAD_DIM - ROPE_DIM), F32)
    zeros = jnp.zeros((t, HEAD_DIM - ROPE_DIM), F32)
    z8 = jnp.zeros((t, ROPE_HALF), F32)
    cos = jnp.concatenate([cos8, cos8, ones], axis=1)
    sa = jnp.concatenate([-sin8, z8, zeros], axis=1)
    sb = jnp.concatenate([z8, sin8, zeros], axis=1)
    rep = LANES // HEAD_DIM
    return jnp.tile(cos, (1, rep)), jnp.tile(sa, (1, rep)), jnp.tile(sb, (1, rep))


def _segment_mean_matrix():
    seg = np.arange(COL_BLOCK) // HEAD_DIM
    return jnp.asarray((seg[:, None] == seg[None, :]) / HEAD_DIM, dtype=BF16)


def _inproj(x, pos, norm_w, w_in_bf, q_norm, k_norm, tm):
    t = x.shape[0]
    cos, sa, sb = _rope_tables(pos)
    hw = jnp.concatenate(
        [jnp.tile(q_norm, ATTN_WIDTH // HEAD_DIM), jnp.tile(k_norm, ATTN_WIDTH // HEAD_DIM)]
    ).reshape(1, 2 * ATTN_WIDTH)
    row = lambda i: (i, 0)
    whole = lambda shape: pl.BlockSpec(shape, lambda i: (0, 0), pipeline_mode=pl.Buffered(1))
    return pl.pallas_call(
        _inproj_kernel,
        grid=(t // tm,),
        in_specs=[
            pl.BlockSpec((tm, D_MODEL), row),
            whole((1, D_MODEL)),
            whole((D_MODEL, IN_WIDTH)),
            whole((1, 2 * ATTN_WIDTH)),
            pl.BlockSpec((tm, LANES), row),
            pl.BlockSpec((tm, LANES), row),
            pl.BlockSpec((tm, LANES), row),
            whole((COL_BLOCK, COL_BLOCK)),
        ],
        out_specs=pl.BlockSpec((tm, Z_WIDTH), row),
        out_shape=jax.ShapeDtypeStruct((t, Z_WIDTH), F32),
        compiler_params=_params("parallel"),
        name="inproj",
    )(x, norm_w.reshape(1, D_MODEL), w_in_bf, hw, cos, sa, sb, _segment_mean_matrix())


HEAD_PAIR = LANES // HEAD_DIM


def _band_kernel(dil, q_ref, kp_ref, kc_ref, vp_ref, vc_ref, o_ref, l_ref):
    mb = pl.program_id(1)
    qi = lax.broadcasted_iota(jnp.int32, (BAND, 2 * BAND), 0)
    kj = lax.broadcasted_iota(jnp.int32, (BAND, 2 * BAND), 1)
    dist = qi + BAND - kj
    valid = (dist >= 0) & (dist <= BAND) & ((kj >= BAND) | (mb > 0))

    def residue(r, carry):
        rows = pl.ds(r, BAND, stride=dil) if dil > 1 else pl.ds(0, BAND)
        q = q_ref[rows, :].astype(BF16)
        k = jnp.concatenate([kp_ref[rows, :], kc_ref[rows, :]], axis=0).astype(BF16)
        v = jnp.concatenate([vp_ref[rows, :], vc_ref[rows, :]], axis=0).astype(BF16)
        outs, lses = [], []
        for h in range(HEAD_PAIR):
            sl = slice(h * HEAD_DIM, (h + 1) * HEAD_DIM)
            s = lax.dot_general(q[:, sl], k[:, sl], NT_DIMS, preferred_element_type=F32) / math.sqrt(HEAD_DIM)
            s = jnp.where(valid, s, NEG)
            m = jnp.max(s, axis=-1, keepdims=True)
            lse = m + jnp.log(jnp.sum(jnp.exp(s - m), axis=-1, keepdims=True))
            p = jnp.exp(s - lse)
            outs.append(jnp.dot(p.astype(BF16), v[:, sl], preferred_element_type=F32))
            lses.append(jnp.broadcast_to(lse, (BAND, HEAD_DIM)))
        o_ref[rows, :] = jnp.concatenate(outs, axis=1)
        l_ref[rows, :] = jnp.concatenate(lses, axis=1)
        return carry

    lax.fori_loop(0, dil, residue, 0)


def _band_attention(z, batch, seq, group, dil):
    span = dil * BAND
    nblk = seq // span
    pairs = GROUP_WIDTH // LANES

    def spec(section, prev):
        base = (section + group * GROUP_WIDTH) // LANES
        if prev:
            return pl.BlockSpec((span, LANES), lambda b, m, p: (b * nblk + jnp.maximum(m - 1, 0), base + p))
        return pl.BlockSpec((span, LANES), lambda b, m, p: (b * nblk + m, base + p))

    out_spec = pl.BlockSpec((span, LANES), lambda b, m, p: (b * nblk + m, p))
    shape = jax.ShapeDtypeStruct((batch * seq, GROUP_WIDTH), F32)
    return pl.pallas_call(
        functools.partial(_band_kernel, dil),
        grid=(batch, nblk, pairs),
        in_specs=[spec(Z_Q, False), spec(Z_K, True), spec(Z_K, False), spec(Z_V, True), spec(Z_V, False)],
        out_specs=[out_spec, out_spec],
        out_shape=[shape, shape],
        compiler_params=_params("parallel", "arbitrary", "arbitrary"),
        name=f"band_attn_d{dil}",
    )(z, z, z, z, z)


KV_WIDTH = 2 * GROUP_WIDTH
WINDOW_BLOCK_BYTES = 4 * 1024 * 1024


def _window_kernel(dil, q_ref, new_ref, c_ref, *refs):
    out_ref, o_ref, l_ref = refs[-3:]
    nb, _, rows = c_ref.shape
    scale = 1.0 / math.sqrt(HEAD_DIM)
    row_id = lax.broadcasted_iota(jnp.int32, (HEADS_PER_GROUP, rows), 1)
    live = (row_id & (dil - 1)) == 0
    is_last = lax.broadcasted_iota(jnp.int32, (KV_WIDTH, rows), 1) == rows - 1
    per_head = lambda a: a.reshape(HEADS_PER_GROUP, HEAD_DIM, a.shape[-1])
    for i in range(nb):
        x = c_ref[i]
        q = q_ref[i]
        new = new_ref[i]
        k, v = x[:GROUP_WIDTH], x[GROUP_WIDTH:]
        s = jnp.where(live, jnp.sum(per_head(k * q), axis=1) * scale, NEG)
        s_new = jnp.sum(per_head(new[:GROUP_WIDTH] * q), axis=1) * scale
        m = jnp.maximum(jnp.max(s, axis=-1, keepdims=True), s_new)
        lse = m + jnp.log(jnp.sum(jnp.exp(s - m), axis=-1, keepdims=True) + jnp.exp(s_new - m))
        p = jnp.exp(s - lse)
        p_new = jnp.exp(s_new - lse)
        spread = lambda a: jnp.broadcast_to(a[:, None, :], (HEADS_PER_GROUP, HEAD_DIM, a.shape[-1])).reshape(
            GROUP_WIDTH, a.shape[-1]
        )
        o_ref[i] = jnp.sum(v * spread(p), axis=-1, keepdims=True) + spread(p_new) * new[GROUP_WIDTH:]
        l_ref[i] = lse
        out_ref[i] = jnp.where(is_last, new, pltpu.roll(x, rows - 1, axis=1))


def _window_update(win, cache_t, layer, z, group, dil):
    depth, n, _, rows = cache_t.shape
    assert rows == dil * BAND, "window buffer shorter than the window is not supported"
    k_new, v_new = _kv_rows(z, group)
    q = z[:, Z_Q + group * GROUP_WIDTH : Z_Q + (group + 1) * GROUP_WIDTH].reshape(n, GROUP_WIDTH, 1)
    new = jnp.concatenate([k_new, v_new], axis=-1).reshape(n, KV_WIDTH, 1)
    nb = max(1, min(n, WINDOW_BLOCK_BYTES // (KV_WIDTH * rows * 4)))
    col = lambda width: pl.BlockSpec((nb, width, 1), lambda i: (i, 0, 0))
    buf = pl.BlockSpec((None, nb, KV_WIDTH, rows), lambda i: (layer, i, 0, 0))
    in_specs = [col(GROUP_WIDTH), col(KV_WIDTH), buf]
    args = [q, new, cache_t]
    aliases = {}
    if win is not None:
        in_specs.append(pl.BlockSpec(memory_space=pl.ANY))
        args.append(win)
        aliases = {3: 0}
    win, o, lse = pl.pallas_call(
        functools.partial(_window_kernel, dil),
        grid=(n // nb,),
        in_specs=in_specs,
        out_specs=[buf, col(GROUP_WIDTH), col(HEADS_PER_GROUP)],
        out_shape=[
            jax.ShapeDtypeStruct(cache_t.shape, F32),
            jax.ShapeDtypeStruct((n, GROUP_WIDTH, 1), F32),
            jax.ShapeDtypeStruct((n, HEADS_PER_GROUP, 1), F32),
        ],
        input_output_aliases=aliases,
        compiler_params=_params("parallel"),
        name=f"window_d{dil}",
    )(*args)
    lse = jnp.broadcast_to(lse, (n, HEADS_PER_GROUP, HEAD_DIM)).reshape(n, GROUP_WIDTH)
    return win, (o.reshape(n, GROUP_WIDTH), lse)


SSM_TILES = SSM_WIDTH // LANES
TILE_STATES = SSM_FLAT // SSM_TILES


def _s5_discretize(ar_ref, ai_ref, ldt_ref, bre_ref, bim_ref, lam_re_ref, lam_im_ref, bbre_ref, bbim_ref):
    ar, ai = ar_ref[...], ai_ref[...]
    dt = jnp.exp(ldt_ref[...])
    mag = jnp.exp(dt * ar)
    lam_re, lam_im = mag * jnp.cos(dt * ai), mag * jnp.sin(dt * ai)
    den = ar * ar + ai * ai
    nr = lam_re - 1.0
    coef_re = (nr * ar + lam_im * ai) / den
    coef_im = (lam_im * ar - nr * ai) / den
    lam_re_ref[...] = jnp.broadcast_to(lam_re, lam_re_ref.shape)
    lam_im_ref[...] = jnp.broadcast_to(lam_im, lam_im_ref.shape)
    for j in range(SSM_TILES):
        sl = slice(j * TILE_STATES, (j + 1) * TILE_STATES)
        br, bi = bre_ref[j], bim_ref[j]
        bbre_ref[j] = (coef_re[:, sl] * br - coef_im[:, sl] * bi).astype(BF16)
        bbim_ref[j] = (coef_re[:, sl] * bi + coef_im[:, sl] * br).astype(BF16)


def _s5_input_map(u, bbre_ref, bbim_ref, xr_ref, xi_ref):
    ub = u.astype(BF16)
    for j in range(SSM_TILES):
        uj = ub[:, j * LANES : (j + 1) * LANES]
        sl = slice(j * TILE_STATES, (j + 1) * TILE_STATES)
        xr_ref[:, sl] = jnp.dot(uj, bbre_ref[j], preferred_element_type=F32)
        xi_ref[:, sl] = jnp.dot(uj, bbim_ref[j], preferred_element_type=F32)


def _s5_output_map(u, xr_ref, xi_ref, cre_ref, cim_ref, d_ref, y_ref):
    for j in range(SSM_TILES):
        sl = slice(j * TILE_STATES, (j + 1) * TILE_STATES)
        ch = slice(j * LANES, (j + 1) * LANES)
        y = (
            jnp.dot(xr_ref[:, sl].astype(BF16), cre_ref[j], preferred_element_type=F32)
            - jnp.dot(xi_ref[:, sl].astype(BF16), cim_ref[j], preferred_element_type=F32)
            + d_ref[:, ch] * u[:, ch]
        )
        y_ref[:, ch] = jax.nn.gelu(y)


SCAN_COLS = 256


def _cmul(ar, ai, br, bi):
    return ar * br - ai * bi, ar * bi + ai * br


def _s5_power_tables(lam_re_ref, lam_im_ref, pw_ref):
    l1 = (lam_re_ref[...], lam_im_ref[...])
    l2 = _cmul(*l1, *l1)
    l4 = _cmul(*l2, *l2)
    sub = lax.broadcasted_iota(jnp.int32, l1[0].shape, 0)
    q = l1
    for bit, p in ((1, l1), (2, l2), (4, l4)):
        nxt = _cmul(*q, *p)
        on = (sub & bit) != 0
        q = (jnp.where(on, nxt[0], q[0]), jnp.where(on, nxt[1], q[1]))
    tables = []
    for s, p in ((1, l1), (2, l2), (4, l4)):
        tables += [jnp.where(sub >= s, p[0], 0.0), jnp.where(sub >= s, p[1], 0.0)]
    tables += [q[0], q[1]]
    for k, t in enumerate(tables):
        pw_ref[k] = t


def _s5_scan_kernel(
    u_ref, ar_ref, ai_ref, ldt_ref, bre_ref, bim_ref, cre_ref, cim_ref, d_ref,
    y_ref, sr_ref, si_ref,
    lam_re_ref, lam_im_ref, bbre_ref, bbim_ref, xr_ref, xi_ref, cr_ref, ci_ref, pw_ref,
):
    c = pl.program_id(1)
    tm = u_ref.shape[0]

    @pl.when(c == 0)
    def _():
        _s5_discretize(ar_ref, ai_ref, ldt_ref, bre_ref, bim_ref, lam_re_ref, lam_im_ref, bbre_ref, bbim_ref)
        _s5_power_tables(lam_re_ref, lam_im_ref, pw_ref)
        cr_ref[...] = jnp.zeros_like(cr_ref)
        ci_ref[...] = jnp.zeros_like(ci_ref)

    u = u_ref[...]
    _s5_input_map(u, bbre_ref, bbim_ref, xr_ref, xi_ref)

    for cb in range(SSM_FLAT // SCAN_COLS):
        cols = slice(cb * SCAN_COLS, (cb + 1) * SCAN_COLS)
        tabs = [pw_ref[k, :, cols] for k in range(8)]

        def block(i, carry):
            rows = pl.ds(pl.multiple_of(i * SUBLANES, SUBLANES), SUBLANES)
            x = (xr_ref[rows, cols], xi_ref[rows, cols])
            for n, s in enumerate((1, 2, 4)):
                prev = (pltpu.roll(x[0], s, axis=0), pltpu.roll(x[1], s, axis=0))
                add = _cmul(tabs[2 * n], tabs[2 * n + 1], *prev)
                x = (x[0] + add[0], x[1] + add[1])
            add = _cmul(tabs[6], tabs[7], *carry)
            x = (x[0] + add[0], x[1] + add[1])
            xr_ref[rows, cols] = x[0]
            xi_ref[rows, cols] = x[1]
            last = slice(SUBLANES - 1, SUBLANES)
            return (jnp.broadcast_to(x[0][last], x[0].shape), jnp.broadcast_to(x[1][last], x[1].shape))

        carry = lax.fori_loop(0, tm // SUBLANES, block, (cr_ref[:, cols], ci_ref[:, cols]))
        cr_ref[:, cols] = carry[0]
        ci_ref[:, cols] = carry[1]

    _s5_output_map(u, xr_ref, xi_ref, cre_ref, cim_ref, d_ref, y_ref)

    @pl.when(c == pl.num_programs(1) - 1)
    def _():
        sr_ref[...] = cr_ref[0:1, :]
        si_ref[...] = ci_ref[0:1, :]


def _s5_step_kernel(
    u_ref, x0r_ref, x0i_ref, ar_ref, ai_ref, ldt_ref, bre_ref, bim_ref, cre_ref, cim_ref, d_ref,
    y_ref, sr_ref, si_ref,
    lam_re_ref, lam_im_ref, bbre_ref, bbim_ref, xr_ref, xi_ref,
):
    _s5_discretize(ar_ref, ai_ref, ldt_ref, bre_ref, bim_ref, lam_re_ref, lam_im_ref, bbre_ref, bbim_ref)
    u = u_ref[...]
    _s5_input_map(u, bbre_ref, bbim_ref, xr_ref, xi_ref)
    lr, li = lam_re_ref[0:1, :], lam_im_ref[0:1, :]
    x0r, x0i = x0r_ref[...], x0i_ref[...]
    xr_ref[...] = lr * x0r - li * x0i + xr_ref[...]
    xi_ref[...] = lr * x0i + li * x0r + xi_ref[...]
    _s5_output_map(u, xr_ref, xi_ref, cre_ref, cim_ref, d_ref, y_ref)
    sr_ref[...] = xr_ref[...]
    si_ref[...] = xi_ref[...]


def _s5_weights(a_re, a_im, log_dt, b_re, b_im, c_re, c_im, d_skip):
    gpt = SSM_GROUPS // SSM_TILES
    eye = jnp.eye(gpt, dtype=F32)

    def expand_b(b):
        b4 = b.astype(F32).reshape(SSM_TILES, gpt, SSM_STATE, SSM_CH)
        return jnp.einsum("jgpc,gh->jgchp", b4, eye).reshape(SSM_TILES, LANES, TILE_STATES)

    def expand_c(cm):
        c4 = cm.astype(F32).reshape(SSM_TILES, gpt, SSM_CH, SSM_STATE)
        return jnp.einsum("jgcp,gh->jhpgc", c4, eye).reshape(SSM_TILES, TILE_STATES, LANES).astype(BF16)

    flat = lambda a: a.astype(F32).reshape(1, SSM_FLAT)
    ldt = jnp.broadcast_to(log_dt.astype(F32)[:, None], (SSM_GROUPS, SSM_STATE)).reshape(1, SSM_FLAT)
    return (
        flat(a_re), flat(a_im), ldt, expand_b(b_re), expand_b(b_im), expand_c(c_re), expand_c(c_im),
        d_skip.astype(F32).reshape(1, SSM_WIDTH),
    )


def _s5_weight_specs(index_map):
    full = lambda shape: pl.BlockSpec(shape, lambda *a: (0,) * len(shape))
    return [
        full((1, SSM_FLAT)), full((1, SSM_FLAT)), full((1, SSM_FLAT)),
        full((SSM_TILES, LANES, TILE_STATES)), full((SSM_TILES, LANES, TILE_STATES)),
        full((SSM_TILES, TILE_STATES, LANES)), full((SSM_TILES, TILE_STATES, LANES)),
        full((1, SSM_WIDTH)),
    ]


def _s5_scratch(rows):
    return [
        pltpu.VMEM((SUBLANES, SSM_FLAT), F32), pltpu.VMEM((SUBLANES, SSM_FLAT), F32),
        pltpu.VMEM((SSM_TILES, LANES, TILE_STATES), BF16), pltpu.VMEM((SSM_TILES, LANES, TILE_STATES), BF16),
        pltpu.VMEM((rows, SSM_FLAT), F32), pltpu.VMEM((rows, SSM_FLAT), F32),
    ]


def _s5_prompt(z, batch, seq, weights, tm):
    nchunk = seq // tm
    ucol = Z_U // SSM_WIDTH
    state_spec = pl.BlockSpec((None, 1, SSM_FLAT), lambda b, c: (b, 0, 0))
    state_shape = jax.ShapeDtypeStruct((batch, 1, SSM_FLAT), F32)
    return pl.pallas_call(
        _s5_scan_kernel,
        grid=(batch, nchunk),
        in_specs=[pl.BlockSpec((tm, SSM_WIDTH), lambda b, c: (b * nchunk + c, ucol))] + _s5_weight_specs(None),
        out_specs=[pl.BlockSpec((tm, SSM_WIDTH), lambda b, c: (b * nchunk + c, 0)), state_spec, state_spec],
        out_shape=[jax.ShapeDtypeStruct((batch * seq, SSM_WIDTH), F32), state_shape, state_shape],
        scratch_shapes=_s5_scratch(tm)
        + [
            pltpu.VMEM((SUBLANES, SSM_FLAT), F32), pltpu.VMEM((SUBLANES, SSM_FLAT), F32),
            pltpu.VMEM((8, SUBLANES, SSM_FLAT), F32),
        ],
        compiler_params=_params("parallel", "arbitrary"),
        name="s5_scan",
    )(z, *weights)


def _s5_sample(z, x0r, x0i, weights):
    n = z.shape[0]
    ucol = Z_U // SSM_WIDTH
    st = pl.BlockSpec((n, SSM_FLAT), lambda i: (0, 0))
    return pl.pallas_call(
        _s5_step_kernel,
        grid=(1,),
        in_specs=[pl.BlockSpec((n, SSM_WIDTH), lambda i: (0, ucol)), st, st] + _s5_weight_specs(None),
        out_specs=[pl.BlockSpec((n, SSM_WIDTH), lambda i: (0, 0)), st, st],
        out_shape=[
            jax.ShapeDtypeStruct((n, SSM_WIDTH), F32),
            jax.ShapeDtypeStruct((n, SSM_FLAT), F32),
            jax.ShapeDtypeStruct((n, SSM_FLAT), F32),
        ],
        scratch_shapes=_s5_scratch(n),
        compiler_params=_params("arbitrary"),
        name="s5_step",
    )(z, x0r, x0i, *weights)


def _merge_kernel(
    x_ref, o1_ref, o2_ref, o3_ref, l1_ref, l2_ref, l3_ref, ys_ref, ga_ref, gb_ref,
    wao_ref, wglu_ref, wout_ref, nf_ref, h_ref, hn_ref,
):
    l1, l2, l3 = l1_ref[...], l2_ref[...], l3_ref[...]
    m = jnp.maximum(jnp.maximum(l1, l2), l3)
    e1, e2, e3 = jnp.exp(l1 - m), jnp.exp(l2 - m), jnp.exp(l3 - m)
    y_attn = (e1 * o1_ref[...] + e2 * o2_ref[...] + e3 * o3_ref[...]) / (e1 + e2 + e3)
    ya = jnp.dot(y_attn.astype(BF16), wao_ref[...], preferred_element_type=F32)
    zg = jnp.dot(ys_ref[...].astype(BF16), wglu_ref[...], preferred_element_type=F32)
    y_ssm = zg[:, :D_MODEL] * jax.nn.sigmoid(zg[:, D_MODEL:])
    mixed = jax.nn.sigmoid(ga_ref[...]) * ya + jax.nn.sigmoid(gb_ref[...]) * y_ssm
    h = x_ref[...] + jnp.dot(mixed.astype(BF16), wout_ref[...], preferred_element_type=F32)
    h_ref[...] = h
    ms = jnp.mean(h * h, axis=-1, keepdims=True)
    hn_ref[...] = (h * lax.rsqrt(ms + EPS) * nf_ref[...]).astype(BF16)


def _merge(x, attn, ys, z, wao_bf, wglu_bf, wout_bf, norm_ffn, tm):
    t = x.shape[0]
    (o1, l1), (o2, l2), (o3, l3) = attn
    row = lambda i: (i, 0)
    grp = pl.BlockSpec((tm, GROUP_WIDTH), row)
    full = lambda shape: pl.BlockSpec(shape, lambda i: (0, 0))
    return pl.pallas_call(
        _merge_kernel,
        grid=(t // tm,),
        in_specs=[
            pl.BlockSpec((tm, D_MODEL), row), grp, grp, grp, grp, grp, grp,
            pl.BlockSpec((tm, SSM_WIDTH), row),
            pl.BlockSpec((tm, D_MODEL), lambda i: (i, Z_GA // D_MODEL)),
            pl.BlockSpec((tm, D_MODEL), lambda i: (i, Z_GB // D_MODEL)),
            full((GROUP_WIDTH, D_MODEL)), full((SSM_WIDTH, 2 * D_MODEL)), full((D_MODEL, D_MODEL)),
            full((1, D_MODEL)),
        ],
        out_specs=[pl.BlockSpec((tm, D_MODEL), row), pl.BlockSpec((tm, D_MODEL), row)],
        out_shape=[jax.ShapeDtypeStruct((t, D_MODEL), F32), jax.ShapeDtypeStruct((t, D_MODEL), BF16)],
        compiler_params=_params("parallel"),
        name="merge",
    )(x, o1, o2, o3, l1, l2, l3, ys, z, z, wao_bf, wglu_bf, wout_bf, norm_ffn.reshape(1, D_MODEL))


def _merge_exchange_pairs(n):
    pairs = []
    t = max(1, math.ceil(math.log2(n)))
    p = 1 << (t - 1)
    while p > 0:
        q, r, d = 1 << (t - 1), 0, p
        while d > 0:
            pairs += [(i, i + d) for i in range(n - d) if (i & p) == r]
            d, q, r = q - p, q >> 1, p
        p >>= 1
    return pairs


SORT16 = tuple(_merge_exchange_pairs(PEER_TOPK))
BITONIC16 = tuple((i, i + d) for d in (8, 4, 2, 1) for i in range(PEER_TOPK) if not i & d)


def _exchange(v, pairs):
    v = list(v)
    for i, j in pairs:
        hi, lo = jnp.maximum(v[i], v[j]), jnp.minimum(v[i], v[j])
        v[i], v[j] = hi, lo
    return v


def _top16_desc(rows):
    v = _exchange(rows, SORT16)
    for shift in (4, 2, 1):
        other = [pltpu.roll(x, shift, axis=0) for x in v]
        v = [jnp.maximum(v[i], other[PEER_TOPK - 1 - i]) for i in range(PEER_TOPK)]
        v = _exchange(v, BITONIC16)
    return v


def _by_sublane(v, start):
    sub = lax.broadcasted_iota(jnp.int32, v[0].shape, 0)
    out = v[start]
    for s in range(1, SUBLANES):
        out = jnp.where(sub == s, v[start + s], out)
    return out


_CAND_ROWS = ((2, 5), (3, 4), (4, 3), (5, 2), (6, 2), (7, 2))


def _top16_of_sums(v1, v2):
    sub = lax.broadcasted_iota(jnp.int32, v1[0].shape, 0)
    v2_lo, v2_hi = _by_sublane(v2, 0), _by_sublane(v2, SUBLANES)
    neg = jnp.full(v1[0].shape, -jnp.inf, F32)
    cand = [v1[0] + v2_lo, v1[0] + v2_hi, v1[1] + v2_lo]
    for a, nb in _CAND_ROWS:
        cand.append(jnp.where(sub < nb, v1[a] + v2_lo, neg))
    cand.append(_by_sublane(v1, SUBLANES) + v2[0])
    cand += [neg] * (PEER_TOPK - len(cand))
    return _top16_desc(cand)


PACKED_KEYS = PEER_NKEYS // 2


def _pack_rows(x):
    return pltpu.bitcast(x.astype(BF16), jnp.uint32)


def _unpack_rows(x):
    return pltpu.bitcast(x, BF16)


def _route_kernel(hn_ref, wq_ref, k1_ref, k2_ref, rank_ref, reach_ref, p2_ref, cf_ref, qt_ref):
    tm = hn_ref.shape[0]
    qt_ref[...] = lax.dot_general(wq_ref[...], hn_ref[...], NT_DIMS, preferred_element_type=F32)

    def head(h, carry):
        q = qt_ref[pl.ds(pl.multiple_of(h * PEER_QDIM, PEER_QDIM), PEER_QDIM), :]
        qn = (q * lax.rsqrt(jnp.mean(q * q, axis=0, keepdims=True) + EPS)).astype(BF16)
        s1 = jnp.dot(k1_ref[...], qn[:PEER_HALF], preferred_element_type=F32)
        s2 = jnp.dot(k2_ref[...], qn[PEER_HALF:], preferred_element_type=F32)
        rows = pl.ds(pl.multiple_of(h * PEER_NKEYS, PEER_NKEYS), PEER_NKEYS)
        packed_rows = pl.ds(pl.multiple_of(h * PACKED_KEYS, PACKED_KEYS), PACKED_KEYS)
        for g in range(tm // LANES):
            lanes = slice(g * LANES, (g + 1) * LANES)
            a, b = s1[:, lanes], s2[:, lanes]
            split = lambda x: [x[i * SUBLANES : (i + 1) * SUBLANES] for i in range(PEER_NKEYS // SUBLANES)]
            v1, v2 = _top16_desc(split(a)), _top16_desc(split(b))
            sc = _top16_of_sums(v1, v2)
            tau = sc[PEER_TOPK - 1][0:1]
            z = sc[0] * 0.0
            for k in range(PEER_TOPK):
                z = z + jnp.exp(sc[k] - sc[0])
            rank2 = jnp.full_like(b, float(PEER_TOPK))
            reach = jnp.full_like(a, float(PEER_TOPK - 1))
            for k in reversed(range(PEER_TOPK)):
                top = v2[k][0:1]
                rank2 = jnp.where(top <= b, float(k), rank2)
                reach = jnp.where(a + top < tau, float(k - 1), reach)
            rank_ref[packed_rows, lanes] = _pack_rows(rank2)
            reach_ref[rows, lanes] = reach
            p2_ref[packed_rows, lanes] = _pack_rows(jnp.exp(b - v2[0][0:1]))
            cf_ref[rows, lanes] = jnp.exp(a - v1[0][0:1]) / z[0:1]
        return carry

    lax.fori_loop(0, PEER_HEADS, head, 0)


def _peer_route(hn, wq_t_bf, k1_bf, k2_bf, tm):
    t = hn.shape[0]
    assert tm % LANES == 0
    spec = lambda rows: pl.BlockSpec((PEER_HEADS * rows, tm), lambda i: (0, i))
    shape = lambda rows, dtype: jax.ShapeDtypeStruct((PEER_HEADS * rows, t), dtype)
    full = lambda shape: pl.BlockSpec(shape, lambda i: (0, 0))
    return pl.pallas_call(
        _route_kernel,
        grid=(t // tm,),
        in_specs=[
            pl.BlockSpec((tm, D_MODEL), lambda i: (i, 0)),
            full((PEER_HEADS * PEER_QDIM, D_MODEL)), full((PEER_NKEYS, PEER_HALF)), full((PEER_NKEYS, PEER_HALF)),
        ],
        out_specs=[spec(PACKED_KEYS), spec(PEER_NKEYS), spec(PACKED_KEYS), spec(PEER_NKEYS)],
        out_shape=[
            shape(PACKED_KEYS, jnp.uint32), shape(PEER_NKEYS, F32), shape(PACKED_KEYS, jnp.uint32), shape(PEER_NKEYS, F32)
        ],
        scratch_shapes=[pltpu.VMEM((PEER_HEADS * PEER_QDIM, tm), F32)],
        compiler_params=_params("parallel"),
        name="peer_route",
    )(hn, wq_t_bf, k1_bf, k2_bf)


EXPERT_ROWS = 8
EXPERT_TILE = EXPERT_ROWS * PEER_NKEYS


def _gate_tile(tile, slot, tm, rank_ref, reach_ref, p2_ref, cf_ref, g_ref):
    for g in range(tm // LANES):
        lanes = slice(g * LANES, (g + 1) * LANES)
        for rr in range(EXPERT_ROWS):
            gate = jnp.zeros((PEER_NKEYS, LANES), BF16)
            for h in range(PEER_HEADS):
                keys = slice(h * PACKED_KEYS, (h + 1) * PACKED_KEYS)
                tile_rows = pl.ds(pl.multiple_of(h * PEER_NKEYS + tile * EXPERT_ROWS, EXPERT_ROWS), EXPERT_ROWS)
                row = lambda ref: jnp.broadcast_to(ref[tile_rows, lanes][rr : rr + 1], (PEER_NKEYS, LANES)).astype(BF16)
                reach, coef = row(reach_ref), row(cf_ref)
                rank2, p2 = _unpack_rows(rank_ref[keys, lanes]), _unpack_rows(p2_ref[keys, lanes])
                gate = gate + jnp.where(rank2 <= reach, p2 * coef, 0)
            g_ref[slot, rr * PACKED_KEYS : (rr + 1) * PACKED_KEYS, lanes] = pltpu.bitcast(gate, jnp.uint32)


def _expert_kernel(hn_ref, h_ref, rank_ref, reach_ref, p2_ref, cf_ref, u_ref, vt_ref, o_ref, acc_ref, g_ref):
    e = pl.program_id(1)
    last = pl.num_programs(1) - 1
    tm = hn_ref.shape[0]
    routing = (rank_ref, reach_ref, p2_ref, cf_ref, g_ref)

    @pl.when(e == 0)
    def _():
        acc_ref[...] = jnp.zeros_like(acc_ref)
        _gate_tile(0, 0, tm, *routing)

    slot = e % 2
    st = lax.dot_general(u_ref[...], hn_ref[...], NT_DIMS, preferred_element_type=F32)
    w = jax.nn.gelu(st).astype(BF16) * _unpack_rows(g_ref[slot])
    acc_ref[...] += jnp.dot(vt_ref[...], w, preferred_element_type=F32)
    _gate_tile(jnp.minimum(e + 1, last), 1 - slot, tm, *routing)

    @pl.when(e == last)
    def _():
        o_ref[...] = h_ref[...] + acc_ref[...].T


def _peer_experts(hn, h, routing, u_bf, vt_bf, tm):
    t = hn.shape[0]
    assert tm % LANES == 0
    tok = pl.BlockSpec((tm, D_MODEL), lambda i, e: (i, 0))
    full = pl.BlockSpec((PEER_HEADS * PEER_NKEYS, tm), lambda i, e: (0, i))
    packed = pl.BlockSpec((PEER_HEADS * PACKED_KEYS, tm), lambda i, e: (0, i))
    return pl.pallas_call(
        _expert_kernel,
        grid=(t // tm, PEER_EXPERTS // EXPERT_TILE),
        in_specs=[
            tok, tok, packed, full, packed, full,
            pl.BlockSpec((EXPERT_TILE, D_MODEL), lambda i, e: (e, 0)),
            pl.BlockSpec((D_MODEL, EXPERT_TILE), lambda i, e: (0, e)),
        ],
        out_specs=tok,
        out_shape=jax.ShapeDtypeStruct((t, D_MODEL), F32),
        scratch_shapes=[pltpu.VMEM((D_MODEL, tm), F32), pltpu.VMEM((2, EXPERT_TILE // 2, tm), jnp.uint32)],
        compiler_params=_params("parallel", "arbitrary"),
        name="peer_experts",
    )(hn, h, *routing, u_bf, vt_bf)


def _kv_rows(z, group):
    k = z[..., Z_K + group * GROUP_WIDTH : Z_K + (group + 1) * GROUP_WIDTH]
    v = z[..., Z_V + group * GROUP_WIDTH : Z_V + (group + 1) * GROUP_WIDTH]
    return k, v


def _tile(t, want):
    return want if t % want == 0 else t


def kernel(x_prompt, x_sample, cache_win1, cache_win2, cache_win3, state_ssm, norm_mix, w_in, q_norm, k_norm, w_attn_out, ssm_a_re, ssm_a_im, ssm_log_dt, ssm_b_re, ssm_b_im, ssm_c_re, ssm_c_im, ssm_d, w_glu, w_out, norm_ffn, peer_w_query, peer_sub_keys1, peer_sub_keys2, peer_u, peer_v):
    caches = (cache_win1, cache_win2, cache_win3)
    batch, seq, _ = x_prompt.shape
    n_req, dec_seq, _ = x_sample.shape
    assert dec_seq == 1, "the sample group appends one token per request"
    depth = w_in.shape[0]
    tp, ts = batch * seq, n_req
    xp = x_prompt.reshape(tp, D_MODEL)
    xs = x_sample.reshape(ts, D_MODEL)
    pos_p = jnp.tile(jnp.arange(seq), batch)
    pos_s = jnp.full((ts,), PAST_LEN, jnp.int32)

    kv_shape = (2, HEADS_PER_GROUP, HEAD_DIM)
    caches_t = [
        jnp.transpose(c, (0, 1, 3, 4, 5, 2)).reshape(depth, n_req, KV_WIDTH, c.shape[2]) for c in caches
    ]
    win_s = [None] * N_GROUPS
    win_p = [[] for _ in ATTN_CONFIGS]
    ssm_p, ssm_s = [], []
    for l in range(depth):
        w_in_bf = w_in[l].astype(BF16)
        wao_bf, wglu_bf, wout_bf = w_attn_out[l].astype(BF16), w_glu[l].astype(BF16), w_out[l].astype(BF16)
        wq_t_bf = peer_w_query[l].astype(BF16).T
        k1_bf, k2_bf = peer_sub_keys1[l].astype(BF16), peer_sub_keys2[l].astype(BF16)
        u_bf, vt_bf = peer_u[l].astype(BF16), peer_v[l].astype(BF16).T
        s5w = _s5_weights(
            ssm_a_re[l], ssm_a_im[l], ssm_log_dt[l], ssm_b_re[l], ssm_b_im[l], ssm_c_re[l], ssm_c_im[l], ssm_d[l]
        )

        zp = _inproj(xp, pos_p, norm_mix[l], w_in_bf, q_norm[l], k_norm[l], _tile(tp, 512))
        attn_p = [_band_attention(zp, batch, seq, g, dil) for g, (_, dil) in enumerate(ATTN_CONFIGS)]
        ys_p, fr, fi = _s5_prompt(zp, batch, seq, s5w, _tile(seq, 512))
        hp, hnp = _merge(xp, attn_p, ys_p, zp, wao_bf, wglu_bf, wout_bf, norm_ffn[l], _tile(tp, 256))
        route_p = _peer_route(hnp, wq_t_bf, k1_bf, k2_bf, _tile(tp, 256))
        xp = _peer_experts(hnp, hp, route_p, u_bf, vt_bf, _tile(tp, 512))
        zp3 = zp.reshape(batch, seq, Z_WIDTH)
        for g, (window, _) in enumerate(ATTN_CONFIGS):
            rows = min(window, seq)
            k, v = _kv_rows(zp3[:, seq - rows :], g)
            shape = (batch, rows, HEADS_PER_GROUP, HEAD_DIM)
            win_p[g].append(jnp.stack([k.reshape(shape), v.reshape(shape)], axis=2))
        ssm_p.append(jnp.stack([fr, fi], axis=-1).reshape(batch, SSM_GROUPS, SSM_STATE, 2))

        zs = _inproj(xs, pos_s, norm_mix[l], w_in_bf, q_norm[l], k_norm[l], ts)
        attn_s = []
        for g, (_, dil) in enumerate(ATTN_CONFIGS):
            win_s[g], attn = _window_update(win_s[g], caches_t[g], l, zs, g, dil)
            attn_s.append(attn)
        st = state_ssm[l].astype(F32).reshape(ts, SSM_FLAT, 2)
        ys_s, nr, ni = _s5_sample(zs, st[..., 0], st[..., 1], s5w)
        hs, hns = _merge(xs, attn_s, ys_s, zs, wao_bf, wglu_bf, wout_bf, norm_ffn[l], ts)
        route_s = _peer_route(hns, wq_t_bf, k1_bf, k2_bf, ts)
        xs = _peer_experts(hns, hs, route_s, u_bf, vt_bf, ts)
        ssm_s.append(jnp.stack([nr, ni], axis=-1).reshape(ts, SSM_GROUPS, SSM_STATE, 2))

    win_s = [
        jnp.transpose(w.reshape(depth, n_req, *kv_shape, w.shape[-1]), (0, 1, 5, 2, 3, 4)) for w in win_s
    ]
    return (
        xp.reshape(batch, seq, D_MODEL),
        xs.reshape(n_req, 1, D_MODEL),
        jnp.stack(win_p[0], axis=0), jnp.stack(win_p[1], axis=0), jnp.stack(win_p[2], axis=0),
        jnp.stack(ssm_p, axis=0),
        win_s[0], win_s[1], win_s[2],
        jnp.stack(ssm_s, axis=0),
    )
```
